```python
import jax, jax.numpy as jnp
from jax import lax
import numpy as np

D_MODEL = 2048
BATCH = 1
SEQ = 8192
DEPTH = 4

N_A_LAYERS = DEPTH // 2
N_B_LAYERS = DEPTH - N_A_LAYERS
EPS = 1e-5
N_MOD = 6

SSM_EXPAND = 2
D_INNER = SSM_EXPAND * D_MODEL
SSM_HEAD_DIM = 64
SSM_HEADS = D_INNER // SSM_HEAD_DIM
SSM_GROUPS = 8
SSM_HEADS_PER_GROUP = SSM_HEADS // SSM_GROUPS
SSM_STATE = 128
CONV_WIDTH = 4
CHUNK = 256
CONV_DIM = D_INNER + 2 * SSM_GROUPS * SSM_STATE
IN_PROJ_DIM = D_INNER + CONV_DIM + SSM_HEADS

DA_HEADS = 8
DA_KV_HEADS = 4
DA_GROUP = DA_HEADS // DA_KV_HEADS
DA_HEAD_DIM = D_MODEL // DA_HEADS // 2
DA_V_DIM = 2 * DA_HEAD_DIM
Q_DIM = DA_HEADS * 2 * DA_HEAD_DIM
K_DIM = DA_KV_HEADS * 2 * DA_HEAD_DIM
V_DIM = DA_KV_HEADS * DA_V_DIM
O_DIM = DA_HEADS * DA_V_DIM
ROPE_DIM = DA_HEAD_DIM // 4
ROPE_THETA = 500000.0
Q_BLOCK = 128

N_EXPERTS = 16
N_EXPERT_GROUPS = 4
EXPERTS_PER_GROUP = N_EXPERTS // N_EXPERT_GROUPS
TOP_K = 2
D_EXPERT = 512

kernel_name = "yoco_mamba2_diffattn_groupmoe_adaln"


def rmsnorm(x, g):
    xf = x.astype(jnp.float32)
    y = xf * lax.rsqrt(jnp.mean(xf * xf, axis=-1, keepdims=True) + EPS)
    return (y * g.astype(jnp.float32)).astype(x.dtype)


def modulate(h, shift, scale):
    return h * (1 + scale[:, None, :]) + shift[:, None, :]


def rope_tables(seq):
    inv = 1.0 / (ROPE_THETA ** (jnp.arange(0, ROPE_DIM, 2, dtype=jnp.float32) / ROPE_DIM))
    ang = jnp.arange(seq, dtype=jnp.float32)[:, None] * inv[None, :]
    return jnp.cos(ang), jnp.sin(ang)


def apply_partial_rope(t, cos, sin):
    half = ROPE_DIM // 2
    shape = (1, t.shape[1]) + (1,) * (t.ndim - 3) + (half,)
    cos = cos.reshape(shape).astype(t.dtype)
    sin = sin.reshape(shape).astype(t.dtype)
    t1, t2 = t[..., :half], t[..., half:ROPE_DIM]
    return jnp.concatenate([t1 * cos - t2 * sin, t2 * cos + t1 * sin, t[..., ROPE_DIM:]], axis=-1)


def causal_dwconv(u, w, b):
    out = lax.conv_general_dilated(
        u, w[:, None, :].astype(u.dtype), window_strides=(1,),
        padding=[(CONV_WIDTH - 1, 0)], dimension_numbers=('NWC', 'WIO', 'NWC'),
        feature_group_count=u.shape[-1])
    return out + b


def ssd_chunked(xdt, dta, bm, cm):
    b, S = xdt.shape[:2]
    pad = (-S) % CHUNK
    if pad:
        padf = lambda t: jnp.pad(t, [(0, 0), (0, pad)] + [(0, 0)] * (t.ndim - 2))
        xdt, dta, bm, cm = padf(xdt), padf(dta), padf(bm), padf(cm)
    nc = (S + pad) // CHUNK
    X = xdt.reshape(b, nc, CHUNK, SSM_GROUPS, SSM_HEADS_PER_GROUP, SSM_HEAD_DIM)
    A = dta.reshape(b, nc, CHUNK, SSM_GROUPS, SSM_HEADS_PER_GROUP).astype(jnp.float32)
    Bc = bm.reshape(b, nc, CHUNK, SSM_GROUPS, SSM_STATE)
    Cc = cm.reshape(b, nc, CHUNK, SSM_GROUPS, SSM_STATE)
    cs = jnp.moveaxis(jnp.cumsum(A, axis=2), 2, -1)
    mask = jnp.tril(jnp.ones((CHUNK, CHUNK), dtype=bool))
    Lmat = jnp.exp(jnp.where(mask, cs[..., :, None] - cs[..., None, :], -jnp.inf))
    CB = jnp.einsum('bclgn,bcsgn->bcgls', Cc, Bc)
    M = CB[:, :, :, None] * Lmat
    y_diag = jnp.einsum('bcgels,bcsgep->bclgep', M, X)
    decay_states = jnp.exp(cs[..., -1:] - cs)
    states = jnp.einsum('bclgn,bcgel,bclgep->bcgepn', Bc, decay_states, X).astype(jnp.float32)
    chunk_decay = jnp.exp(cs[..., -1])

    def step(h, inp):
        s_c, d_c = inp
        return h * d_c[..., None, None] + s_c, h

    h0 = jnp.zeros((b, SSM_GROUPS, SSM_HEADS_PER_GROUP, SSM_HEAD_DIM, SSM_STATE), jnp.float32)
    _, prev = lax.scan(step, h0, (jnp.moveaxis(states, 1, 0), jnp.moveaxis(chunk_decay, 1, 0)))
    prev = jnp.moveaxis(prev, 0, 1)
    y_off = jnp.einsum('bclgn,bcgepn,bcgel->bclgep', Cc, prev, jnp.exp(cs))
    y = (y_diag + y_off).reshape(b, nc * CHUNK, SSM_GROUPS, SSM_HEADS_PER_GROUP, SSM_HEAD_DIM)
    return y[:, :S]


def mamba2_mixer(u, in_w, conv_w, conv_b, dt_bias, a_log, d_skip, norm_g, out_w):
    b, S, _ = u.shape
    zxbcdt = u @ in_w
    z = zxbcdt[..., :D_INNER]
    xbc = zxbcdt[..., D_INNER:D_INNER + CONV_DIM]
    dt = zxbcdt[..., D_INNER + CONV_DIM:]
    xbc = jax.nn.silu(causal_dwconv(xbc, conv_w, conv_b))
    gn = SSM_GROUPS * SSM_STATE
    xs = xbc[..., :D_INNER].reshape(b, S, SSM_GROUPS, SSM_HEADS_PER_GROUP, SSM_HEAD_DIM)
    bm = xbc[..., D_INNER:D_INNER + gn].reshape(b, S, SSM_GROUPS, SSM_STATE)
    cm = xbc[..., D_INNER + gn:].reshape(b, S, SSM_GROUPS, SSM_STATE)
    dt = jax.nn.softplus(dt.astype(jnp.float32) + dt_bias.astype(jnp.float32))
    dt = dt.reshape(b, S, SSM_GROUPS, SSM_HEADS_PER_GROUP)
    A = -jnp.exp(a_log.astype(jnp.float32)).reshape(SSM_GROUPS, SSM_HEADS_PER_GROUP)
    y = ssd_chunked(xs * dt[..., None], dt * A, bm, cm)
    y = y + xs * d_skip.reshape(SSM_GROUPS, SSM_HEADS_PER_GROUP)[:, :, None]
    y = y.reshape(b, S, D_INNER) * jax.nn.silu(z.astype(jnp.float32))
    y = rmsnorm(y.reshape(b, S, SSM_GROUPS, D_INNER // SSM_GROUPS),
                norm_g.reshape(SSM_GROUPS, D_INNER // SSM_GROUPS)).reshape(b, S, D_INNER)
    return y.astype(u.dtype) @ out_w


def diff_attention(u, k, v, w_q, lam_q1, lam_k1, lam_q2, lam_k2, subln_g, w_o, lambda_init, cos, sin):
    b, S, _ = u.shape
    q = (u @ w_q).reshape(b, S, DA_KV_HEADS, DA_GROUP, 2, DA_HEAD_DIM)
    q = apply_partial_rope(q, cos, sin) * (DA_HEAD_DIM ** -0.5)
    lam = (jnp.exp(jnp.sum(lam_q1.astype(jnp.float32) * lam_k1.astype(jnp.float32)))
           - jnp.exp(jnp.sum(lam_q2.astype(jnp.float32) * lam_k2.astype(jnp.float32)))
           + lambda_init)
    nb = S // Q_BLOCK
    qb = jnp.moveaxis(q.reshape(b, nb, Q_BLOCK, DA_KV_HEADS, DA_GROUP, 2, DA_HEAD_DIM), 1, 0)
    kpos = jnp.arange(S)

    def block(args):
        qi, i = args
        qpos = i * Q_BLOCK + jnp.arange(Q_BLOCK)
        s = jnp.einsum('bqhgtd,bshtd->bhgtqs', qi, k).astype(jnp.float32)
        s = jnp.where(kpos[None, :] <= qpos[:, None], s, -jnp.inf)
        a = jax.nn.softmax(s, axis=-1)
        a = a[:, :, :, 0] - lam * a[:, :, :, 1]
        return jnp.einsum('bhgqs,bshv->bqhgv', a.astype(v.dtype), v)

    o = lax.map(block, (qb, jnp.arange(nb)))
    o = jnp.moveaxis(o, 0, 1).reshape(b, S, DA_HEADS, DA_V_DIM)
    o = rmsnorm(o, subln_g) * (1.0 - lambda_init)
    return o.reshape(b, S, O_DIM) @ w_o


def group_moe(u, router_w, router_b, w_gate, w_up, w_down):
    b, S, _ = u.shape
    scores = jax.nn.sigmoid((u @ router_w).astype(jnp.float32))
    sel = scores + router_b.astype(jnp.float32)
    sel_g = sel.reshape(b, S, N_EXPERT_GROUPS, EXPERTS_PER_GROUP)
    group_score = jnp.sum(lax.top_k(sel_g, TOP_K)[0], axis=-1)
    best_group = jnp.argmax(group_score, axis=-1)
    gmask = jnp.arange(N_EXPERT_GROUPS) == best_group[..., None]
    masked = jnp.where(gmask[..., None], sel_g, -jnp.inf).reshape(b, S, N_EXPERTS)
    _, idx = lax.top_k(masked, TOP_K)
    w = jnp.take_along_axis(scores, idx, axis=-1)
    w = w / jnp.sum(w, axis=-1, keepdims=True)
    gates = jnp.sum(jax.nn.one_hot(idx, N_EXPERTS, dtype=jnp.float32) * w[..., None], axis=-2)
    h = jax.nn.silu(jnp.einsum('bsd,edf->bsef', u, w_gate)) * jnp.einsum('bsd,edf->bsef', u, w_up)
    h = h * gates[..., None].astype(h.dtype)
    return jnp.einsum('bsef,efd->bsd', h, w_down)


def setup_inputs(seed: int = 0) -> dict:
    key = jax.random.key(seed)
    ks = iter(jax.random.split(key, 40))
    nrm = lambda shape, s: jax.random.normal(next(ks), shape, jnp.float32) * s
    gain = lambda shape: 1.0 + nrm(shape, 0.01)
    D = D_MODEL
    u = jax.random.uniform(next(ks), (N_A_LAYERS, SSM_HEADS), jnp.float32)
    dt = jnp.clip(jnp.exp(u * (np.log(0.1) - np.log(0.001)) + np.log(0.001)), 1e-4)
    dt_bias = dt + jnp.log(-jnp.expm1(-dt))
    a_log = jnp.log(jax.random.uniform(next(ks), (N_A_LAYERS, SSM_HEADS), jnp.float32, 1.0, 16.0))
    return {
        "x": nrm((BATCH, SEQ, D), 1.0),
        "c": nrm((BATCH, D), 1.0),
        "mod_w": nrm((DEPTH, D, N_MOD * D), 0.5 * D ** -0.5),
        "mod_b": nrm((DEPTH, N_MOD * D), 0.02),
        "norm1_g": gain((DEPTH, D)),
        "norm2_g": gain((DEPTH, D)),
        "ssm_in_w": nrm((N_A_LAYERS, D, IN_PROJ_DIM), D ** -0.5),
        "ssm_conv_w": nrm((N_A_LAYERS, CONV_WIDTH, CONV_DIM), CONV_WIDTH ** -0.5),
        "ssm_conv_b": nrm((N_A_LAYERS, CONV_DIM), 0.02),
        "ssm_dt_bias": dt_bias,
        "ssm_a_log": a_log,
        "ssm_d": 1.0 + nrm((N_A_LAYERS, SSM_HEADS), 0.1),
        "ssm_norm_g": gain((N_A_LAYERS, D_INNER)),
        "ssm_out_w": nrm((N_A_LAYERS, D_INNER, D), D_INNER ** -0.5),
        "kv_mod_w": nrm((D, 2 * D), 0.5 * D ** -0.5),
        "kv_mod_b": nrm((2 * D,), 0.02),
        "kv_norm_g": gain((D,)),
        "w_kv": nrm((D, K_DIM + V_DIM), D ** -0.5),
        "attn_q_w": nrm((N_B_LAYERS, D, Q_DIM), D ** -0.5),
        "lam_q1": nrm((N_B_LAYERS, DA_HEAD_DIM), 0.1),
        "lam_k1": nrm((N_B_LAYERS, DA_HEAD_DIM), 0.1),
        "lam_q2": nrm((N_B_LAYERS, DA_HEAD_DIM), 0.1),
        "lam_k2": nrm((N_B_LAYERS, DA_HEAD_DIM), 0.1),
        "subln_g": gain((N_B_LAYERS, DA_V_DIM)),
        "attn_o_w": nrm((N_B_LAYERS, O_DIM, D), O_DIM ** -0.5),
        "moe_w_gate": nrm((DEPTH, N_EXPERTS, D, D_EXPERT), D ** -0.5),
        "moe_w_up": nrm((DEPTH, N_EXPERTS, D, D_EXPERT), D ** -0.5),
        "moe_w_down": nrm((DEPTH, N_EXPERTS, D_EXPERT, D), D_EXPERT ** -0.5),
        "router_w": nrm((D, N_EXPERTS), D ** -0.5),
        "router_b": nrm((N_EXPERTS,), 0.01),
        "final_g": gain((D,)),
    }


def reference(x, c, mod_w, mod_b, norm1_g, norm2_g, ssm_in_w, ssm_conv_w, ssm_conv_b,
              ssm_dt_bias, ssm_a_log, ssm_d, ssm_norm_g, ssm_out_w, kv_mod_w, kv_mod_b,
              kv_norm_g, w_kv, attn_q_w, lam_q1, lam_k1, lam_q2, lam_k2, subln_g, attn_o_w,
              moe_w_gate, moe_w_up, moe_w_down, router_w, router_b, final_g):
    b, S, _ = x.shape
    cos, sin = rope_tables(S)
    sc = jax.nn.silu(c)
    k = v = None
    for layer in range(DEPTH):
        if layer == N_A_LAYERS:
            kv_shift, kv_scale = jnp.split(sc @ kv_mod_w + kv_mod_b, 2, axis=-1)
            kv = modulate(rmsnorm(x, kv_norm_g), kv_shift, kv_scale) @ w_kv
            k = apply_partial_rope(kv[..., :K_DIM].reshape(b, S, DA_KV_HEADS, 2, DA_HEAD_DIM), cos, sin)
            v = kv[..., K_DIM:].reshape(b, S, DA_KV_HEADS, DA_V_DIM)
        sh1, sc1, g1, sh2, sc2, g2 = jnp.split(sc @ mod_w[layer] + mod_b[layer], N_MOD, axis=-1)
        h = modulate(rmsnorm(x, norm1_g[layer]), sh1, sc1)
        if layer < N_A_LAYERS:
            i = layer
            mix = mamba2_mixer(h, ssm_in_w[i], ssm_conv_w[i], ssm_conv_b[i], ssm_dt_bias[i],
                               ssm_a_log[i], ssm_d[i], ssm_norm_g[i], ssm_out_w[i])
        else:
            j = layer - N_A_LAYERS
            lambda_init = 0.8 - 0.6 * float(np.exp(-0.3 * layer))
            mix = diff_attention(h, k, v, attn_q_w[j], lam_q1[j], lam_k1[j], lam_q2[j], lam_k2[j],
                                 subln_g[j], attn_o_w[j], lambda_init, cos, sin)
        x = x + g1[:, None, :] * mix
        h = modulate(rmsnorm(x, norm2_g[layer]), sh2, sc2)
        x = x + g2[:, None, :] * group_moe(h, router_w, router_b, moe_w_gate[layer],
                                            moe_w_up[layer], moe_w_down[layer])
    return rmsnorm(x, final_g)
```

```python
import functools

import numpy as np
import jax
import jax.numpy as jnp
from jax import lax
from jax.experimental import pallas as pl
from jax.experimental.pallas import tpu as pltpu

F32 = jnp.float32
BF16 = jnp.bfloat16

D_MODEL = 2048
DEPTH = 4
N_A_LAYERS = DEPTH // 2
EPS = 1e-5
N_MOD = 6

D_INNER = 2 * D_MODEL
SSM_HEAD_DIM = 64
SSM_HEADS = D_INNER // SSM_HEAD_DIM
SSM_GROUPS = 8
SSM_HPG = SSM_HEADS // SSM_GROUPS
SSM_STATE = 128
CONV_WIDTH = 4
CHUNK = 256
GROUP_W = SSM_HPG * SSM_HEAD_DIM
ZX_DIM = 2 * D_INNER + 2 * SSM_GROUPS * SSM_STATE

DA_HEADS = 8
DA_KV_HEADS = 4
DA_GROUP = DA_HEADS // DA_KV_HEADS
DA_HEAD_DIM = D_MODEL // DA_HEADS // 2
DA_V_DIM = 2 * DA_HEAD_DIM
K_DIM = DA_KV_HEADS * 2 * DA_HEAD_DIM
V_DIM = DA_KV_HEADS * DA_V_DIM
ROPE_DIM = DA_HEAD_DIM // 4
ROPE_HALF = ROPE_DIM // 2
ROPE_THETA = 500000.0

N_EXPERTS = 16
N_EXPERT_GROUPS = 4
EXPERTS_PER_GROUP = N_EXPERTS // N_EXPERT_GROUPS
TOP_K = 2
D_EXPERT = 512

LANES = 128
HALO = 16
VMEM_LIMIT = 48 * 1024 * 1024

PROJ_TM = 1024
PROJ_TN = 512
ATT_T = 512
MOE_TM = 256
ROUTER_TM = 512
COMBINE_TM = 512


def _cparams(*sem):
    return pltpu.CompilerParams(dimension_semantics=sem, vmem_limit_bytes=VMEM_LIMIT)


def _sigmoid(v):
    return 1.0 / (1.0 + jnp.exp(-v))


def _lane_tile(v, n):
    return v if n == 1 else jnp.concatenate([v] * n, axis=1)


def _norm_mod(x, g, sh, sc):
    ms = jnp.mean(x * x, axis=-1, keepdims=True)
    return (x * lax.rsqrt(ms + EPS) * g) * (1.0 + sc) + sh


def _split3(v):
    hi = v.astype(BF16)
    r1 = v - hi.astype(F32)
    mid = r1.astype(BF16)
    lo = (r1 - mid.astype(F32)).astype(BF16)
    return hi, mid, lo


def _dot(a, b):
    return jnp.dot(a, b, preferred_element_type=F32)


def _dot3(v, sel):
    hi, mid, lo = _split3(v)
    return _dot(hi, sel) + _dot(mid, sel) + _dot(lo, sel)


def _dot3_l(sel, v):
    hi, mid, lo = _split3(v)
    return _dot(sel, hi) + _dot(sel, mid) + _dot(sel, lo)


def _mod_kernel(c_ref, w_ref, b_ref, o_ref):
    cc = c_ref[...]
    sc = cc * _sigmoid(cc)
    w = w_ref[...]
    o_ref[...] = jnp.sum(w * _lane_tile(sc, w.shape[1] // LANES), axis=0, keepdims=True) + b_ref[...]


def _mod_call(cb, w, b):
    L, D, N = w.shape
    tn = 1024
    return pl.pallas_call(
        _mod_kernel,
        grid=(L, N // tn),
        in_specs=[pl.BlockSpec((D, LANES), lambda l, j: (0, 0)),
                  pl.BlockSpec((None, D, tn), lambda l, j: (l, 0, j)),
                  pl.BlockSpec((None, 1, tn), lambda l, j: (l, 0, j))],
        out_specs=pl.BlockSpec((None, 1, tn), lambda l, j: (l, 0, j)),
        out_shape=jax.ShapeDtypeStruct((L, 1, N), F32),
        compiler_params=_cparams("parallel", "parallel"),
        name="mod_vectors",
    )(cb, w, b.reshape(L, 1, N))


def _norm_prologue(x_ref, g_ref, sh_ref, sc_ref, h_ref):
    tm = x_ref.shape[0]
    rc = 256
    for r in range(0, tm, rc):
        h = _norm_mod(x_ref[r:r + rc, :], g_ref[...], sh_ref[...], sc_ref[...])
        h_ref[r:r + rc, :] = h.astype(BF16)


def _in_proj_kernel(x_ref, g_ref, sh_ref, sc_ref, w_ref, wdt_ref, dtb_ref, o_ref, dt_ref, h_ref):
    @pl.when(pl.program_id(1) == 0)
    def _():
        _norm_prologue(x_ref, g_ref, sh_ref, sc_ref, h_ref)
        d = _dot(h_ref[...], wdt_ref[...]) + dtb_ref[...]
        dt_ref[...] = jnp.maximum(d, 0.0) + jnp.log1p(jnp.exp(-jnp.abs(d)))

    o_ref[...] = _dot(h_ref[...], w_ref[...]).astype(o_ref.dtype)


def _in_proj_call(x, g, sh, sc, w, wdt, dtb):
    M, K = x.shape
    N = w.shape[1]
    tm, tn = PROJ_TM, PROJ_TN
    vec = pl.BlockSpec((1, K), lambda i, j: (0, 0))
    return pl.pallas_call(
        _in_proj_kernel,
        grid=(M // tm, N // tn),
        in_specs=[pl.BlockSpec((tm, K), lambda i, j: (i, 0)), vec, vec, vec,
                  pl.BlockSpec((K, tn), lambda i, j: (0, j)),
                  pl.BlockSpec((K, LANES), lambda i, j: (0, 0)),
                  pl.BlockSpec((1, LANES), lambda i, j: (0, 0))],
        out_specs=[pl.BlockSpec((tm, tn), lambda i, j: (i, j)),
                   pl.BlockSpec((tm, LANES), lambda i, j: (i, 0))],
        out_shape=[jax.ShapeDtypeStruct((M, N), BF16), jax.ShapeDtypeStruct((M, LANES), F32)],
        scratch_shapes=[pltpu.VMEM((tm, K), BF16)],
        compiler_params=_cparams("parallel", "arbitrary"),
        name="ssm_in_proj",
    )(x, g, sh, sc, w, wdt, dtb)


def _rope_proj_kernel(x_ref, g_ref, sh_ref, sc_ref, w_ref, cos_ref, sa_ref, sb_ref, o_ref, h_ref,
                      *, rope_tiles, n_tiles, scale):
    j = pl.program_id(1)

    @pl.when(j == 0)
    def _():
        _norm_prologue(x_ref, g_ref, sh_ref, sc_ref, h_ref)

    acc = _dot(h_ref[...], w_ref[...])
    tn = acc.shape[1]
    nb = tn // LANES

    def roped():
        cos = _lane_tile(cos_ref[...], nb)
        sa = _lane_tile(sa_ref[...], nb)
        sb = _lane_tile(sb_ref[...], nb)
        r = acc * cos + pltpu.roll(acc, tn - ROPE_HALF, 1) * sa + pltpu.roll(acc, ROPE_HALF, 1) * sb
        return (r * scale).astype(o_ref.dtype)

    if rope_tiles >= n_tiles:
        o_ref[...] = roped()
    else:
        @pl.when(j < rope_tiles)
        def _():
            o_ref[...] = roped()

        @pl.when(j >= rope_tiles)
        def _():
            o_ref[...] = acc.astype(o_ref.dtype)


def _rope_proj_call(x, g, sh, sc, w, cos, sa, sb, *, rope_cols, scale, name):
    M, K = x.shape
    N = w.shape[1]
    tm, tn = PROJ_TM, PROJ_TN
    vec = pl.BlockSpec((1, K), lambda i, j: (0, 0))
    tab = pl.BlockSpec((tm, LANES), lambda i, j: (i, 0))
    kern = functools.partial(_rope_proj_kernel, rope_tiles=rope_cols // tn, n_tiles=N // tn, scale=scale)
    return pl.pallas_call(
        kern,
        grid=(M // tm, N // tn),
        in_specs=[pl.BlockSpec((tm, K), lambda i, j: (i, 0)), vec, vec, vec,
                  pl.BlockSpec((K, tn), lambda i, j: (0, j)), tab, tab, tab],
        out_specs=pl.BlockSpec((tm, tn), lambda i, j: (i, j)),
        out_shape=jax.ShapeDtypeStruct((M, N), BF16),
        scratch_shapes=[pltpu.VMEM((tm, K), BF16)],
        compiler_params=_cparams("parallel", "arbitrary"),
        name=name,
    )(x, g, sh, sc, w, cos, sa, sb)


def _res_mm_kernel(a_ref, w_ref, res_ref, gate_ref, o_ref):
    o_ref[...] = res_ref[...] + gate_ref[...] * _dot(a_ref[...], w_ref[...])


def _res_mm_call(a, w, res, gate, *, name):
    M, K = a.shape
    N = w.shape[1]
    tm, tn = PROJ_TM, PROJ_TN
    return pl.pallas_call(
        _res_mm_kernel,
        grid=(M // tm, N // tn),
        in_specs=[pl.BlockSpec((tm, K), lambda i, j: (i, 0)),
                  pl.BlockSpec((K, tn), lambda i, j: (0, j)),
                  pl.BlockSpec((tm, tn), lambda i, j: (i, j)),
                  pl.BlockSpec((1, tn), lambda i, j: (0, j))],
        out_specs=pl.BlockSpec((tm, tn), lambda i, j: (i, j)),
        out_shape=jax.ShapeDtypeStruct((M, N), F32),
        compiler_params=_cparams("parallel", "parallel"),
        name=name,
    )(a, w, res, gate)


def _ssd_kernel(z_ref, x_ref, xh_ref, b_ref, bh_ref, c_ref, ch_ref, dt_ref, alog_ref,
                cwx_ref, cbx_ref, cwb_ref, cbb_ref, cwc_ref, cbc_ref, dsk_ref, ng_ref,
                o_ref, h_ref, cst_ref):
    g = pl.program_id(0)
    c = pl.program_id(1)
    L = CHUNK

    @pl.when(c == 0)
    def _():
        h_ref[...] = jnp.zeros_like(h_ref)

    def conv_silu(cur_ref, halo_ref, w_ref, bias_ref):
        cur = cur_ref[...].astype(F32)
        prev = jnp.where(c == 0, 0.0, halo_ref[...].astype(F32))
        ext = jnp.concatenate([prev, cur], axis=0)
        w = w_ref[...]
        acc = bias_ref[...] + w[CONV_WIDTH - 1:CONV_WIDTH, :] * cur
        for k in range(1, CONV_WIDTH):
            sh = pltpu.roll(ext, k, 0)[HALO:HALO + L, :]
            acc = acc + w[CONV_WIDTH - 1 - k:CONV_WIDTH - k, :] * sh
        return acc * _sigmoid(acc)

    xs = conv_silu(x_ref, xh_ref, cwx_ref, cbx_ref)
    bm = conv_silu(b_ref, bh_ref, cwb_ref, cbb_ref)
    cm = conv_silu(c_ref, ch_ref, cwc_ref, cbc_ref)

    dt_all = dt_ref[...]
    dta = dt_all * (-jnp.exp(alog_ref[...]))
    row = lax.broadcasted_iota(jnp.int32, (L, L), 0)
    col = lax.broadcasted_iota(jnp.int32, (L, L), 1)
    causal = row >= col
    tril = jnp.where(causal, 1.0, 0.0).astype(BF16)
    cs_all = _dot3_l(tril, dta)
    cst_ref[...] = cs_all.T
    cs_r = cst_ref[pl.ds(pl.multiple_of(g * SSM_HPG, SSM_HPG), SSM_HPG), :]

    k1 = lax.broadcasted_iota(jnp.int32, (LANES, LANES), 0)
    n1 = lax.broadcasted_iota(jnp.int32, (LANES, LANES), 1)
    sel = jnp.where(jnp.where(n1 < SSM_HPG, k1 - n1, -1) == g * SSM_HPG, 1.0, 0.0).astype(BF16)
    cs_g = _dot3(cs_all, sel)
    k2 = lax.broadcasted_iota(jnp.int32, (LANES, GROUP_W), 0)
    n2 = lax.broadcasted_iota(jnp.int32, (LANES, GROUP_W), 1) // SSM_HEAD_DIM
    esel = jnp.where(k2 - n2 == g * SSM_HPG, 1.0, 0.0).astype(BF16)
    cs_x = _dot3(cs_all, esel)
    dt_x = _dot3(dt_all, esel)

    xdt = xs * dt_x
    xb = xdt.astype(BF16)
    bb = bm.astype(BF16)
    cb = cm.astype(BF16)
    cbm = lax.dot_general(cb, bb, (((1,), (1,)), ((), ())), preferred_element_type=F32)

    half = GROUP_W // 2
    lane_head = lax.broadcasted_iota(jnp.int32, (L, half), 1) // SSM_HEAD_DIM
    yd = []
    for blk in range(2):
        xblk = xb[:, blk * half:(blk + 1) * half]
        acc = jnp.zeros((L, half), F32)
        for e4 in range(SSM_HPG // 2):
            e = blk * (SSM_HPG // 2) + e4
            diff = cs_g[:, e:e + 1] - cs_r[e:e + 1, :]
            lm = jnp.exp(jnp.where(causal, diff, -jnp.inf))
            m = (cbm * lm).astype(BF16)
            xm = jnp.where(lane_head == e4, xblk, jnp.zeros_like(xblk))
            acc = acc + _dot(m, xm)
        yd.append(acc)
    y = jnp.concatenate(yd, axis=1)

    hprev = h_ref[...]
    y = y + _dot(cb, hprev.astype(BF16)) * jnp.exp(cs_x)
    cs_last = cs_x[L - 1:L, :]
    xdec = (xdt * jnp.exp(cs_last - cs_x)).astype(BF16)
    states = lax.dot_general(bb, xdec, (((0,), (0,)), ((), ())), preferred_element_type=F32)
    h_ref[...] = hprev * jnp.exp(cs_last) + states

    y = y + xs * dsk_ref[...]
    zz = z_ref[...].astype(F32)
    y = y * (zz * _sigmoid(zz))
    ms = jnp.mean(y * y, axis=-1, keepdims=True)
    o_ref[...] = (y * lax.rsqrt(ms + EPS) * ng_ref[...]).astype(o_ref.dtype)


def _ssd_call(zx, dt, alog, conv_w, conv_b, dskip, norm_g):
    S = zx.shape[0]
    G, L, W, N = SSM_GROUPS, CHUNK, GROUP_W, SSM_STATE
    nc = S // L
    xo = D_INNER // W
    bo = 2 * D_INNER // N
    co = bo + G
    hb = L // HALO

    def halo(off):
        return lambda g, c: (jnp.maximum(c * hb - 1, 0), off + g)

    cw = conv_w
    cb = conv_b.reshape(1, -1)
    cxo, cbo, cco = 0, D_INNER // N, D_INNER // N + G
    in_specs = [
        pl.BlockSpec((L, W), lambda g, c: (c, g)),
        pl.BlockSpec((L, W), lambda g, c: (c, xo + g)),
        pl.BlockSpec((HALO, W), halo(xo)),
        pl.BlockSpec((L, N), lambda g, c: (c, bo + g)),
        pl.BlockSpec((HALO, N), halo(bo)),
        pl.BlockSpec((L, N), lambda g, c: (c, co + g)),
        pl.BlockSpec((HALO, N), halo(co)),
        pl.BlockSpec((L, LANES), lambda g, c: (c, 0)),
        pl.BlockSpec((1, LANES), lambda g, c: (0, 0)),
        pl.BlockSpec((CONV_WIDTH, W), lambda g, c: (0, cxo + g)),
        pl.BlockSpec((1, W), lambda g, c: (0, cxo + g)),
        pl.BlockSpec((CONV_WIDTH, N), lambda g, c: (0, cbo + g)),
        pl.BlockSpec((1, N), lambda g, c: (0, cbo + g)),
        pl.BlockSpec((CONV_WIDTH, N), lambda g, c: (0, cco + g)),
        pl.BlockSpec((1, N), lambda g, c: (0, cco + g)),
        pl.BlockSpec((1, W), lambda g, c: (0, g)),
        pl.BlockSpec((1, W), lambda g, c: (0, g)),
    ]
    return pl.pallas_call(
        _ssd_kernel,
        grid=(G, nc),
        in_specs=in_specs,
        out_specs=pl.BlockSpec((L, W), lambda g, c: (c, g)),
        out_shape=jax.ShapeDtypeStruct((S, D_INNER), BF16),
        scratch_shapes=[pltpu.VMEM((N, W), F32), pltpu.VMEM((LANES, L), F32)],
        compiler_params=_cparams("parallel", "arbitrary"),
        name="ssd_chunk_scan",
    )(zx, zx, zx, zx, zx, zx, zx, dt, alog, cw, cb, cw, cb, cw, cb, dskip, norm_g)


def _attn_kernel(qi_ref, kj_ref, q_ref, k_ref, v_ref, lq1_ref, lk1_ref, lq2_ref, lk2_ref, sg_ref,
                 o_ref, m_ref, l_ref, acc_ref, *, lambda_init):
    s_idx = pl.program_id(1)
    qi = qi_ref[s_idx]
    kj = kj_ref[s_idx]
    T = ATT_T
    HD = DA_HEAD_DIM

    @pl.when(kj == 0)
    def _():
        m_ref[...] = jnp.full_like(m_ref, -jnp.inf)
        l_ref[...] = jnp.zeros_like(l_ref)
        acc_ref[...] = jnp.zeros_like(acc_ref)

    def step(masked):
        v = v_ref[...]
        if masked:
            row = lax.broadcasted_iota(jnp.int32, (T, T), 0)
            col = lax.broadcasted_iota(jnp.int32, (T, T), 1)
            keep = col <= row
        for t in range(2):
            kt = k_ref[:, t * HD:(t + 1) * HD]
            for gq in range(DA_GROUP):
                idx = gq * 2 + t
                q = q_ref[:, idx * HD:(idx + 1) * HD]
                s = lax.dot_general(q, kt, (((1,), (1,)), ((), ())), preferred_element_type=F32)
                if masked:
                    s = jnp.where(keep, s, -jnp.inf)
                m_prev = m_ref[idx]
                m_new = jnp.maximum(m_prev, jnp.max(s, axis=-1, keepdims=True))
                alpha = jnp.exp(m_prev - m_new)
                p = jnp.exp(s - m_new)
                l_ref[idx] = alpha * l_ref[idx] + jnp.sum(p, axis=-1, keepdims=True)
                acc_ref[idx] = alpha * acc_ref[idx] + _dot(p.astype(BF16), v)
                m_ref[idx] = m_new

    @pl.when(kj < qi)
    def _():
        step(False)

    @pl.when(kj == qi)
    def _():
        step(True)
        lam = (jnp.exp(jnp.sum(lq1_ref[...] * lk1_ref[...], axis=-1, keepdims=True))
               - jnp.exp(jnp.sum(lq2_ref[...] * lk2_ref[...], axis=-1, keepdims=True)) + lambda_init)
        for gq in range(DA_GROUP):
            a0 = acc_ref[gq * 2] / l_ref[gq * 2]
            a1 = acc_ref[gq * 2 + 1] / l_ref[gq * 2 + 1]
            o = a0 - lam * a1
            ms = jnp.mean(o * o, axis=-1, keepdims=True)
            o = (o * lax.rsqrt(ms + EPS) * sg_ref[...]) * (1.0 - lambda_init)
            o_ref[:, gq * DA_V_DIM:(gq + 1) * DA_V_DIM] = o.astype(o_ref.dtype)


def _attn_call(q, kv, lq1, lk1, lq2, lk2, subln_g, lambda_init):
    S = q.shape[0]
    v_off = K_DIM // DA_V_DIM
    T = ATT_T
    nq = S // T
    pairs = [(i, j) for i in range(nq) for j in range(i + 1)]
    qi_tab = jnp.asarray(np.array([p[0] for p in pairs], np.int32))
    kj_tab = jnp.asarray(np.array([p[1] for p in pairs], np.int32))
    qw = DA_GROUP * 2 * DA_HEAD_DIM
    kw = 2 * DA_HEAD_DIM
    vec = pl.BlockSpec((1, DA_HEAD_DIM), lambda h, s, qt, kt: (0, 0))
    grid_spec = pltpu.PrefetchScalarGridSpec(
        num_scalar_prefetch=2,
        grid=(DA_KV_HEADS, len(pairs)),
        in_specs=[pl.BlockSpec((T, qw), lambda h, s, qt, kt: (qt[s], h)),
                  pl.BlockSpec((T, kw), lambda h, s, qt, kt: (kt[s], h)),
                  pl.BlockSpec((T, DA_V_DIM), lambda h, s, qt, kt: (kt[s], v_off + h)),
                  vec, vec, vec, vec,
                  pl.BlockSpec((1, DA_V_DIM), lambda h, s, qt, kt: (0, 0))],
        out_specs=pl.BlockSpec((T, DA_GROUP * DA_V_DIM), lambda h, s, qt, kt: (qt[s], h)),
        scratch_shapes=[pltpu.VMEM((2 * DA_GROUP, T, 1), F32),
                        pltpu.VMEM((2 * DA_GROUP, T, 1), F32),
                        pltpu.VMEM((2 * DA_GROUP, T, DA_V_DIM), F32)],
    )
    return pl.pallas_call(
        functools.partial(_attn_kernel, lambda_init=lambda_init),
        grid_spec=grid_spec,
        out_shape=jax.ShapeDtypeStruct((S, DA_HEADS * DA_V_DIM), BF16),
        compiler_params=_cparams("parallel", "arbitrary"),
        name="diff_attention",
    )(qi_tab, kj_tab, q, kv, kv, lq1, lk1, lq2, lk2, subln_g)


def _router_kernel(x_ref, g_ref, sh_ref, sc_ref, rw_ref, rb_ref, h_ref, idx_ref, wt_ref):
    h = _norm_mod(x_ref[...], g_ref[...], sh_ref[...], sc_ref[...])
    h_ref[...] = h.astype(BF16)
    hh = h.astype(BF16)
    hl = (h - hh.astype(F32)).astype(BF16)
    rw = rw_ref[...]
    wh = rw.astype(BF16)
    wl = (rw - wh.astype(F32)).astype(BF16)
    logits = _dot(hh, wh) + _dot(hl, wh) + _dot(hh, wl)
    lt = logits.T[0:N_EXPERTS, :]
    score = _sigmoid(lt)
    sel = score + rb_ref[...]
    srow = [score[e:e + 1, :] for e in range(N_EXPERTS)]
    row = [sel[e:e + 1, :] for e in range(N_EXPERTS)]

    gscore = []
    for gi in range(N_EXPERT_GROUPS):
        r = row[gi * EXPERTS_PER_GROUP:(gi + 1) * EXPERTS_PER_GROUP]
        best = None
        for a in range(EXPERTS_PER_GROUP):
            for b in range(a + 1, EXPERTS_PER_GROUP):
                s = r[a] + r[b]
                best = s if best is None else jnp.maximum(best, s)
        gscore.append(best)
    bestg = jnp.zeros_like(gscore[0], dtype=jnp.int32)
    bestv = gscore[0]
    for gi in range(1, N_EXPERT_GROUPS):
        upd = gscore[gi] > bestv
        bestg = jnp.where(upd, gi, bestg)
        bestv = jnp.where(upd, gscore[gi], bestv)

    cand = [jnp.where(bestg == e // EXPERTS_PER_GROUP, row[e], -jnp.inf) for e in range(N_EXPERTS)]

    def first_argmax(vals):
        bi = jnp.zeros_like(bestg)
        bv = vals[0]
        for e in range(1, N_EXPERTS):
            upd = vals[e] > bv
            bi = jnp.where(upd, e, bi)
            bv = jnp.where(upd, vals[e], bv)
        return bi

    i1 = first_argmax(cand)
    i2 = first_argmax([jnp.where(i1 == e, -jnp.inf, cand[e]) for e in range(N_EXPERTS)])
    s1 = sum(jnp.where(i1 == e, srow[e], 0.0) for e in range(N_EXPERTS))
    s2 = sum(jnp.where(i2 == e, srow[e], 0.0) for e in range(N_EXPERTS))
    den = s1 + s2
    pad_i = jnp.zeros((6, i1.shape[1]), jnp.int32)
    pad_f = jnp.zeros((6, i1.shape[1]), F32)
    idx_ref[...] = jnp.concatenate([i1, i2, pad_i], axis=0)
    wt_ref[...] = jnp.concatenate([s1 / den, s2 / den, pad_f], axis=0)


def _router_call(x, g, sh, sc, rw_pad, rb_col):
    S, D = x.shape
    tm = ROUTER_TM
    vec = pl.BlockSpec((1, D), lambda i: (0, 0))
    return pl.pallas_call(
        _router_kernel,
        grid=(S // tm,),
        in_specs=[pl.BlockSpec((tm, D), lambda i: (i, 0)), vec, vec, vec,
                  pl.BlockSpec((D, LANES), lambda i: (0, 0)),
                  pl.BlockSpec((N_EXPERTS, 1), lambda i: (0, 0))],
        out_specs=[pl.BlockSpec((tm, D), lambda i: (i, 0)),
                   pl.BlockSpec((8, tm), lambda i: (0, i)),
                   pl.BlockSpec((8, tm), lambda i: (0, i))],
        out_shape=[jax.ShapeDtypeStruct((S, D), BF16),
                   jax.ShapeDtypeStruct((8, S), jnp.int32),
                   jax.ShapeDtypeStruct((8, S), F32)],
        compiler_params=_cparams("parallel"),
        name="moe_norm_router",
    )(x, g, sh, sc, rw_pad, rb_col)


def _moe_kernel(ti_ref, te_ref, tf_ref, nv_ref, x_ref, wr_ref, wg_ref, wu_ref, wd_ref, o_ref,
                wgb_ref, wub_ref, wdb_ref):
    t = pl.program_id(0)

    @pl.when(t < nv_ref[0])
    def _():
        @pl.when(tf_ref[t] == 1)
        def _():
            wgb_ref[...] = wg_ref[...].astype(BF16)
            wub_ref[...] = wu_ref[...].astype(BF16)
            wdb_ref[...] = wd_ref[...].astype(BF16)

        x = x_ref[...]
        gt = _dot(x, wgb_ref[...])
        up = _dot(x, wub_ref[...])
        h = (gt * _sigmoid(gt)) * up
        h = h * _lane_tile(wr_ref[...], D_EXPERT // LANES)
        o_ref[...] = _dot(h.astype(BF16), wdb_ref[...]).astype(o_ref.dtype)


def _moe_call(xs, wrow, tile_idx, tile_expert, tile_first, n_valid, w_gate, w_up, w_down, layer):
    R, D = xs.shape
    tm = MOE_TM
    F = D_EXPERT
    n_tiles = R // tm
    grid_spec = pltpu.PrefetchScalarGridSpec(
        num_scalar_prefetch=4,
        grid=(n_tiles,),
        in_specs=[pl.BlockSpec((tm, D), lambda t, ti, te, tf, nv: (ti[t], 0)),
                  pl.BlockSpec((tm, LANES), lambda t, ti, te, tf, nv: (ti[t], 0)),
                  pl.BlockSpec((None, None, D, F), lambda t, ti, te, tf, nv: (layer, te[t], 0, 0)),
                  pl.BlockSpec((None, None, D, F), lambda t, ti, te, tf, nv: (layer, te[t], 0, 0)),
                  pl.BlockSpec((None, None, F, D), lambda t, ti, te, tf, nv: (layer, te[t], 0, 0))],
        out_specs=pl.BlockSpec((tm, D), lambda t, ti, te, tf, nv: (ti[t], 0)),
        scratch_shapes=[pltpu.VMEM((D, F), BF16), pltpu.VMEM((D, F), BF16), pltpu.VMEM((F, D), BF16)],
    )
    return pl.pallas_call(
        _moe_kernel,
        grid_spec=grid_spec,
        out_shape=jax.ShapeDtypeStruct((R, D), BF16),
        compiler_params=_cparams("arbitrary"),
        name="moe_experts",
    )(tile_idx, tile_expert, tile_first, n_valid, xs, wrow, w_gate, w_up, w_down)


def _combine_kernel(x_ref, y1_ref, y2_ref, g_ref, fg_ref, o_ref, *, final):
    x = x_ref[...] + g_ref[...] * (y1_ref[...].astype(F32) + y2_ref[...].astype(F32))
    if final:
        ms = jnp.mean(x * x, axis=-1, keepdims=True)
        x = x * lax.rsqrt(ms + EPS) * fg_ref[...]
    o_ref[...] = x


def _combine_call(x, y1, y2, gate, final_g, final):
    S, D = x.shape
    tm = COMBINE_TM
    blk = pl.BlockSpec((tm, D), lambda i: (i, 0))
    vec = pl.BlockSpec((1, D), lambda i: (0, 0))
    return pl.pallas_call(
        functools.partial(_combine_kernel, final=final),
        grid=(S // tm,),
        in_specs=[blk, blk, blk, vec, vec],
        out_specs=blk,
        out_shape=jax.ShapeDtypeStruct((S, D), F32),
        compiler_params=_cparams("parallel"),
        name="moe_combine",
    )(x, y1, y2, gate, final_g)


def _dispatch_plan(idx, wts, S):
    tm = MOE_TM
    n_tiles = (TOP_K * S) // tm + N_EXPERTS
    R = n_tiles * tm
    e_flat = idx[:TOP_K].reshape(-1)
    w_flat = wts[:TOP_K].reshape(-1)
    tok_flat = jnp.tile(jnp.arange(S, dtype=jnp.int32), TOP_K)
    onehot = (e_flat[:, None] == jnp.arange(N_EXPERTS, dtype=jnp.int32)[None, :]).astype(jnp.int32)
    incl = jnp.cumsum(onehot, axis=0)
    counts = incl[-1]
    rank = jnp.sum((incl - 1) * onehot, axis=1)
    tiles_e = (counts + tm - 1) // tm
    tile_end = jnp.cumsum(tiles_e)
    tile_start = tile_end - tiles_e
    dest = (tile_start * tm)[e_flat] + rank
    src_tok = jnp.zeros((R,), jnp.int32).at[dest].set(tok_flat)
    wrow = jnp.zeros((R,), F32).at[dest].set(w_flat)
    n_valid = tile_end[-1]
    t_ids = jnp.minimum(jnp.arange(n_tiles, dtype=jnp.int32), n_valid - 1)
    tile_expert = jnp.sum((t_ids[:, None] >= tile_end[None, :]).astype(jnp.int32), axis=1)
    tile_first = (t_ids == tile_start[tile_expert]).astype(jnp.int32)
    return (src_tok, jnp.broadcast_to(wrow[:, None], (R, LANES)), t_ids.astype(jnp.int32),
            tile_expert.astype(jnp.int32), tile_first, n_valid.reshape(1).astype(jnp.int32),
            dest[:S], dest[S:])


def _moe_block(x, g, sh, sc, gate, rw_pad, rb_col, w_gate, w_up, w_down, layer, final_g, final):
    S = x.shape[0]
    h, idx, wts = _router_call(x, g, sh, sc, rw_pad, rb_col)
    src_tok, wrow, t_ids, t_exp, t_first, n_valid, pos1, pos2 = _dispatch_plan(idx, wts, S)
    xs = jnp.take(h, src_tok, axis=0)
    ys = _moe_call(xs, wrow, t_ids, t_exp, t_first, n_valid, w_gate, w_up, w_down, layer)
    y1 = jnp.take(ys, pos1, axis=0)
    y2 = jnp.take(ys, pos2, axis=0)
    return _combine_call(x, y1, y2, gate, final_g, final)


def _rope_tables(S):
    inv = 1.0 / (ROPE_THETA ** (jnp.arange(0, ROPE_DIM, 2, dtype=F32) / ROPE_DIM))
    ang = jnp.arange(S, dtype=F32)[:, None] * inv[None, :]
    cos, sin = jnp.cos(ang), jnp.sin(ang)
    zeros = jnp.zeros((S, LANES - ROPE_DIM), F32)
    zh = jnp.zeros((S, ROPE_HALF), F32)
    cos_t = jnp.concatenate([cos, cos, jnp.ones((S, LANES - ROPE_DIM), F32)], axis=1)
    sin_a = jnp.concatenate([-sin, zh, zeros], axis=1)
    sin_b = jnp.concatenate([zh, sin, zeros], axis=1)
    return cos_t, sin_a, sin_b


def kernel(x, c, mod_w, mod_b, norm1_g, norm2_g, ssm_in_w, ssm_conv_w, ssm_conv_b, ssm_dt_bias, ssm_a_log, ssm_d, ssm_norm_g, ssm_out_w, kv_mod_w, kv_mod_b, kv_norm_g, w_kv, attn_q_w, lam_q1, lam_k1, lam_q2, lam_k2, subln_g, attn_o_w, moe_w_gate, moe_w_up, moe_w_down, router_w, router_b, final_g):
    _, S, D = x.shape
    xr = x[0]
    cb = jnp.broadcast_to(c[0][:, None], (D, LANES))
    mods = _mod_call(cb, mod_w, mod_b)
    kvm = _mod_call(cb, kv_mod_w[None], kv_mod_b[None])
    cos_t, sin_a, sin_b = _rope_tables(S)
    rw_pad = jnp.pad(router_w, ((0, 0), (0, LANES - N_EXPERTS)))
    rb_col = router_b.reshape(N_EXPERTS, 1)
    row = lambda v: v.reshape(1, -1)
    pad_l = lambda v: jnp.pad(v.reshape(1, -1), ((0, 0), (0, LANES - v.shape[-1])))

    kv = None
    for layer in range(DEPTH):
        m = mods[layer]
        sh1, sc1, g1, sh2, sc2, g2 = [m[:, i * D:(i + 1) * D] for i in range(N_MOD)]
        if layer == N_A_LAYERS:
            kv_sh, kv_sc = kvm[0][:, :D], kvm[0][:, D:]
            kv = _rope_proj_call(xr, row(kv_norm_g), kv_sh, kv_sc, w_kv.astype(BF16), cos_t, sin_a, sin_b,
                                 rope_cols=K_DIM, scale=1.0, name="kv_proj")
        if layer < N_A_LAYERS:
            i = layer
            w_in = ssm_in_w[i]
            zx, dt = _in_proj_call(xr, row(norm1_g[layer]), sh1, sc1,
                                   w_in[:, :ZX_DIM].astype(BF16),
                                   jnp.pad(w_in[:, ZX_DIM:], ((0, 0), (0, LANES - SSM_HEADS))).astype(BF16),
                                   pad_l(ssm_dt_bias[i]))
            y = _ssd_call(zx, dt, pad_l(ssm_a_log[i]), ssm_conv_w[i], ssm_conv_b[i],
                          row(jnp.repeat(ssm_d[i], SSM_HEAD_DIM)), row(ssm_norm_g[i]))
            xr = _res_mm_call(y, ssm_out_w[i].astype(BF16), xr, g1, name="ssm_out_proj")
        else:
            j = layer - N_A_LAYERS
            lambda_init = 0.8 - 0.6 * float(np.exp(-0.3 * layer))
            q = _rope_proj_call(xr, row(norm1_g[layer]), sh1, sc1, attn_q_w[j].astype(BF16),
                                cos_t, sin_a, sin_b, rope_cols=D, scale=DA_HEAD_DIM ** -0.5, name="q_proj")
            o = _attn_call(q, kv, row(lam_q1[j]), row(lam_k1[j]), row(lam_q2[j]), row(lam_k2[j]),
                           row(subln_g[j]), lambda_init)
            xr = _res_mm_call(o, attn_o_w[j].astype(BF16), xr, g1, name="attn_o_proj")
        xr = _moe_block(xr, row(norm2_g[layer]), sh2, sc2, g2, rw_pad, rb_col,
                        moe_w_gate, moe_w_up, moe_w_down, layer, row(final_g), layer == DEPTH - 1)
    return xr[None]
```

```python
import functools

import numpy as np
import jax
import jax.numpy as jnp
from jax import lax
from jax.experimental import pallas as pl
from jax.experimental.pallas import tpu as pltpu

F32 = jnp.float32
BF16 = jnp.bfloat16

D_MODEL = 2048
DEPTH = 4
N_A_LAYERS = DEPTH // 2
EPS = 1e-5
N_MOD = 6

D_INNER = 2 * D_MODEL
SSM_HEAD_DIM = 64
SSM_HEADS = D_INNER // SSM_HEAD_DIM
SSM_GROUPS = 8
SSM_HPG = SSM_HEADS // SSM_GROUPS
SSM_STATE = 128
CONV_WIDTH = 4
CHUNK = 256
GROUP_W = SSM_HPG * SSM_HEAD_DIM
ZX_DIM = 2 * D_INNER + 2 * SSM_GROUPS * SSM_STATE

DA_HEADS = 8
DA_KV_HEADS = 4
DA_GROUP = DA_HEADS // DA_KV_HEADS
DA_HEAD_DIM = D_MODEL // DA_HEADS // 2
DA_V_DIM = 2 * DA_HEAD_DIM
K_DIM = DA_KV_HEADS * 2 * DA_HEAD_DIM
V_DIM = DA_KV_HEADS * DA_V_DIM
ROPE_DIM = DA_HEAD_DIM // 4
ROPE_HALF = ROPE_DIM // 2
ROPE_THETA = 500000.0

N_EXPERTS = 16
N_EXPERT_GROUPS = 4
EXPERTS_PER_GROUP = N_EXPERTS // N_EXPERT_GROUPS
TOP_K = 2
D_EXPERT = 512

LOG2E = 1.4426950408889634
LANES = 128
HALO = 16
VMEM_LIMIT = 48 * 1024 * 1024

PROJ_TM = 1024
PROJ_TN = 512
ATT_T = 512
MOE_TM = 256
ROUTER_TM = 512
COMBINE_TM = 512


def _cparams(*sem):
    return pltpu.CompilerParams(dimension_semantics=sem, vmem_limit_bytes=VMEM_LIMIT)


def _sigmoid(v):
    return 1.0 / (1.0 + jnp.exp(-v))


def _lane_tile(v, n):
    return v if n == 1 else jnp.concatenate([v] * n, axis=1)


def _norm_mod(x, g, sh, sc):
    ms = jnp.mean(x * x, axis=-1, keepdims=True)
    return (x * lax.rsqrt(ms + EPS) * g) * (1.0 + sc) + sh


def _split3(v):
    hi = v.astype(BF16)
    r1 = v - hi.astype(F32)
    mid = r1.astype(BF16)
    lo = (r1 - mid.astype(F32)).astype(BF16)
    return hi, mid, lo


def _dot(a, b):
    return jnp.dot(a, b, preferred_element_type=F32)


def _dot_parts(parts, sel):
    acc = _dot(parts[0], sel)
    for p in parts[1:]:
        acc = acc + _dot(p, sel)
    return acc


def _dot3_l(sel, v):
    hi, mid, lo = _split3(v)
    return _dot(sel, hi) + _dot(sel, mid) + _dot(sel, lo)


def _mod_kernel(c_ref, w_ref, b_ref, o_ref):
    cc = c_ref[...]
    sc = cc * _sigmoid(cc)
    w = w_ref[...]
    o_ref[...] = jnp.sum(w * _lane_tile(sc, w.shape[1] // LANES), axis=0, keepdims=True) + b_ref[...]


def _mod_call(cb, w, b):
    L, D, N = w.shape
    tn = 1024
    return pl.pallas_call(
        _mod_kernel,
        grid=(L, N // tn),
        in_specs=[pl.BlockSpec((D, LANES), lambda l, j: (0, 0)),
                  pl.BlockSpec((None, D, tn), lambda l, j: (l, 0, j)),
                  pl.BlockSpec((None, 1, tn), lambda l, j: (l, 0, j))],
        out_specs=pl.BlockSpec((None, 1, tn), lambda l, j: (l, 0, j)),
        out_shape=jax.ShapeDtypeStruct((L, 1, N), F32),
        compiler_params=_cparams("parallel", "parallel"),
        name="mod_vectors",
    )(cb, w, b.reshape(L, 1, N))


def _norm_prologue(x_ref, g_ref, sh_ref, sc_ref, h_ref):
    tm = x_ref.shape[0]
    rc = 256
    for r in range(0, tm, rc):
        h = _norm_mod(x_ref[r:r + rc, :], g_ref[...], sh_ref[...], sc_ref[...])
        h_ref[r:r + rc, :] = h.astype(BF16)


def _in_proj_kernel(x_ref, g_ref, sh_ref, sc_ref, w_ref, wdt_ref, dtb_ref, o_ref, dt_ref, h_ref):
    @pl.when(pl.program_id(1) == 0)
    def _():
        _norm_prologue(x_ref, g_ref, sh_ref, sc_ref, h_ref)
        d = _dot(h_ref[...], wdt_ref[...]) + dtb_ref[...]
        dt_ref[...] = jnp.maximum(d, 0.0) + jnp.log1p(jnp.exp(-jnp.abs(d)))

    o_ref[...] = _dot(h_ref[...], w_ref[...]).astype(o_ref.dtype)


def _in_proj_call(x, g, sh, sc, w, wdt, dtb):
    M, K = x.shape
    N = w.shape[1]
    tm, tn = PROJ_TM, PROJ_TN
    vec = pl.BlockSpec((1, K), lambda i, j: (0, 0))
    return pl.pallas_call(
        _in_proj_kernel,
        grid=(M // tm, N // tn),
        in_specs=[pl.BlockSpec((tm, K), lambda i, j: (i, 0)), vec, vec, vec,
                  pl.BlockSpec((K, tn), lambda i, j: (0, j)),
                  pl.BlockSpec((K, LANES), lambda i, j: (0, 0)),
                  pl.BlockSpec((1, LANES), lambda i, j: (0, 0))],
        out_specs=[pl.BlockSpec((tm, tn), lambda i, j: (i, j)),
                   pl.BlockSpec((tm, LANES), lambda i, j: (i, 0))],
        out_shape=[jax.ShapeDtypeStruct((M, N), BF16), jax.ShapeDtypeStruct((M, LANES), F32)],
        scratch_shapes=[pltpu.VMEM((tm, K), BF16)],
        compiler_params=_cparams("parallel", "arbitrary"),
        name="ssm_in_proj",
    )(x, g, sh, sc, w, wdt, dtb)


def _rope_proj_kernel(x_ref, g_ref, sh_ref, sc_ref, w_ref, cos_ref, sa_ref, sb_ref, o_ref, h_ref,
                      *, rope_tiles, n_tiles, scale):
    j = pl.program_id(1)

    @pl.when(j == 0)
    def _():
        _norm_prologue(x_ref, g_ref, sh_ref, sc_ref, h_ref)

    acc = _dot(h_ref[...], w_ref[...])
    tn = acc.shape[1]
    nb = tn // LANES

    def roped():
        cos = _lane_tile(cos_ref[...], nb)
        sa = _lane_tile(sa_ref[...], nb)
        sb = _lane_tile(sb_ref[...], nb)
        r = acc * cos + pltpu.roll(acc, tn - ROPE_HALF, 1) * sa + pltpu.roll(acc, ROPE_HALF, 1) * sb
        return (r * scale).astype(o_ref.dtype)

    if rope_tiles >= n_tiles:
        o_ref[...] = roped()
    else:
        @pl.when(j < rope_tiles)
        def _():
            o_ref[...] = roped()

        @pl.when(j >= rope_tiles)
        def _():
            o_ref[...] = acc.astype(o_ref.dtype)


def _rope_proj_call(x, g, sh, sc, w, cos, sa, sb, *, rope_cols, scale, name):
    M, K = x.shape
    N = w.shape[1]
    tm, tn = PROJ_TM, PROJ_TN
    vec = pl.BlockSpec((1, K), lambda i, j: (0, 0))
    tab = pl.BlockSpec((tm, LANES), lambda i, j: (i, 0))
    kern = functools.partial(_rope_proj_kernel, rope_tiles=rope_cols // tn, n_tiles=N // tn, scale=scale)
    return pl.pallas_call(
        kern,
        grid=(M // tm, N // tn),
        in_specs=[pl.BlockSpec((tm, K), lambda i, j: (i, 0)), vec, vec, vec,
                  pl.BlockSpec((K, tn), lambda i, j: (0, j)), tab, tab, tab],
        out_specs=pl.BlockSpec((tm, tn), lambda i, j: (i, j)),
        out_shape=jax.ShapeDtypeStruct((M, N), BF16),
        scratch_shapes=[pltpu.VMEM((tm, K), BF16)],
        compiler_params=_cparams("parallel", "arbitrary"),
        name=name,
    )(x, g, sh, sc, w, cos, sa, sb)


def _res_mm_kernel(a_ref, w_ref, res_ref, gate_ref, o_ref):
    o_ref[...] = res_ref[...] + gate_ref[...] * _dot(a_ref[...], w_ref[...])


def _res_mm_call(a, w, res, gate, *, name):
    M, K = a.shape
    N = w.shape[1]
    tm, tn = PROJ_TM, PROJ_TN
    return pl.pallas_call(
        _res_mm_kernel,
        grid=(M // tm, N // tn),
        in_specs=[pl.BlockSpec((tm, K), lambda i, j: (i, 0)),
                  pl.BlockSpec((K, tn), lambda i, j: (0, j)),
                  pl.BlockSpec((tm, tn), lambda i, j: (i, j)),
                  pl.BlockSpec((1, tn), lambda i, j: (0, j))],
        out_specs=pl.BlockSpec((tm, tn), lambda i, j: (i, j)),
        out_shape=jax.ShapeDtypeStruct((M, N), F32),
        compiler_params=_cparams("parallel", "parallel"),
        name=name,
    )(a, w, res, gate)


def _ssd_kernel(z_ref, x_ref, xh_ref, b_ref, bh_ref, c_ref, ch_ref, dt_ref, alog_ref,
                cwx_ref, cbx_ref, cwb_ref, cbb_ref, cwc_ref, cbc_ref, dsk_ref, ng_ref,
                o_ref, h_ref, cst_ref):
    g = pl.program_id(0)
    c = pl.program_id(1)
    L = CHUNK

    @pl.when(c == 0)
    def _():
        h_ref[...] = jnp.zeros_like(h_ref)

    def conv_silu(cur_ref, halo_ref, w_ref, bias_ref):
        cur = cur_ref[...].astype(F32)
        prev = jnp.where(c == 0, 0.0, halo_ref[...].astype(F32))
        ext = jnp.concatenate([prev, cur], axis=0)
        w = w_ref[...]
        acc = bias_ref[...] + w[CONV_WIDTH - 1:CONV_WIDTH, :] * cur
        for k in range(1, CONV_WIDTH):
            sh = pltpu.roll(ext, k, 0)[HALO:HALO + L, :]
            acc = acc + w[CONV_WIDTH - 1 - k:CONV_WIDTH - k, :] * sh
        return acc * _sigmoid(acc)

    xs = conv_silu(x_ref, xh_ref, cwx_ref, cbx_ref)
    bm = conv_silu(b_ref, bh_ref, cwb_ref, cbb_ref)
    cm = conv_silu(c_ref, ch_ref, cwc_ref, cbc_ref)

    dt_all = dt_ref[...]
    dta = dt_all * (-jnp.exp(alog_ref[...]))
    row = lax.broadcasted_iota(jnp.int32, (L, L), 0)
    col = lax.broadcasted_iota(jnp.int32, (L, L), 1)
    causal = row >= col
    tril = jnp.where(causal, 1.0, 0.0).astype(BF16)
    cs_all = _dot3_l(tril, dta)
    cst_ref[...] = cs_all.T
    cs_r = cst_ref[pl.ds(pl.multiple_of(g * SSM_HPG, SSM_HPG), SSM_HPG), :]

    k1 = lax.broadcasted_iota(jnp.int32, (LANES, LANES), 0)
    n1 = lax.broadcasted_iota(jnp.int32, (LANES, LANES), 1)
    sel = jnp.where(jnp.where(n1 < SSM_HPG, k1 - n1, -1) == g * SSM_HPG, 1.0, 0.0).astype(BF16)
    cs_parts = _split3(cs_all)
    cs_g = _dot_parts(cs_parts, sel)
    k2 = lax.broadcasted_iota(jnp.int32, (LANES, GROUP_W), 0)
    n2 = lax.broadcasted_iota(jnp.int32, (LANES, GROUP_W), 1) // SSM_HEAD_DIM
    esel = jnp.where(k2 - n2 == g * SSM_HPG, 1.0, 0.0).astype(BF16)
    cs_x = _dot_parts(cs_parts, esel)
    dt_x = _dot_parts(_split3(dt_all)[:2], esel)

    xdt = xs * dt_x
    xb = xdt.astype(BF16)
    bb = bm.astype(BF16)
    cb = cm.astype(BF16)
    cbm = lax.dot_general(cb, bb, (((1,), (1,)), ((), ())), preferred_element_type=F32)

    half = GROUP_W // 2
    lane_head = lax.broadcasted_iota(jnp.int32, (L, half), 1) // SSM_HEAD_DIM
    yd = []
    for blk in range(2):
        xblk = xb[:, blk * half:(blk + 1) * half]
        acc = jnp.zeros((L, half), F32)
        for e4 in range(SSM_HPG // 2):
            e = blk * (SSM_HPG // 2) + e4
            diff = cs_g[:, e:e + 1] - cs_r[e:e + 1, :]
            lm = jnp.exp(jnp.where(causal, diff, -jnp.inf))
            m = (cbm * lm).astype(BF16)
            xm = jnp.where(lane_head == e4, xblk, jnp.zeros_like(xblk))
            acc = acc + _dot(m, xm)
        yd.append(acc)
    y = jnp.concatenate(yd, axis=1)

    hprev = h_ref[...]
    y = y + _dot(cb, hprev.astype(BF16)) * jnp.exp(cs_x)
    cs_last = cs_x[L - 1:L, :]
    xdec = (xdt * jnp.exp(cs_last - cs_x)).astype(BF16)
    states = lax.dot_general(bb, xdec, (((0,), (0,)), ((), ())), preferred_element_type=F32)
    h_ref[...] = hprev * jnp.exp(cs_last) + states

    y = y + xs * dsk_ref[...]
    zz = z_ref[...].astype(F32)
    y = y * (zz * _sigmoid(zz))
    ms = jnp.mean(y * y, axis=-1, keepdims=True)
    o_ref[...] = (y * lax.rsqrt(ms + EPS) * ng_ref[...]).astype(o_ref.dtype)


def _ssd_call(zx, dt, alog, conv_w, conv_b, dskip, norm_g):
    S = zx.shape[0]
    G, L, W, N = SSM_GROUPS, CHUNK, GROUP_W, SSM_STATE
    nc = S // L
    xo = D_INNER // W
    bo = 2 * D_INNER // N
    co = bo + G
    hb = L // HALO

    def halo(off):
        return lambda g, c: (jnp.maximum(c * hb - 1, 0), off + g)

    cw = conv_w
    cb = conv_b.reshape(1, -1)
    cxo, cbo, cco = 0, D_INNER // N, D_INNER // N + G
    in_specs = [
        pl.BlockSpec((L, W), lambda g, c: (c, g)),
        pl.BlockSpec((L, W), lambda g, c: (c, xo + g)),
        pl.BlockSpec((HALO, W), halo(xo)),
        pl.BlockSpec((L, N), lambda g, c: (c, bo + g)),
        pl.BlockSpec((HALO, N), halo(bo)),
        pl.BlockSpec((L, N), lambda g, c: (c, co + g)),
        pl.BlockSpec((HALO, N), halo(co)),
        pl.BlockSpec((L, LANES), lambda g, c: (c, 0)),
        pl.BlockSpec((1, LANES), lambda g, c: (0, 0)),
        pl.BlockSpec((CONV_WIDTH, W), lambda g, c: (0, cxo + g)),
        pl.BlockSpec((1, W), lambda g, c: (0, cxo + g)),
        pl.BlockSpec((CONV_WIDTH, N), lambda g, c: (0, cbo + g)),
        pl.BlockSpec((1, N), lambda g, c: (0, cbo + g)),
        pl.BlockSpec((CONV_WIDTH, N), lambda g, c: (0, cco + g)),
        pl.BlockSpec((1, N), lambda g, c: (0, cco + g)),
        pl.BlockSpec((1, W), lambda g, c: (0, g)),
        pl.BlockSpec((1, W), lambda g, c: (0, g)),
    ]
    return pl.pallas_call(
        _ssd_kernel,
        grid=(G, nc),
        in_specs=in_specs,
        out_specs=pl.BlockSpec((L, W), lambda g, c: (c, g)),
        out_shape=jax.ShapeDtypeStruct((S, D_INNER), BF16),
        scratch_shapes=[pltpu.VMEM((N, W), F32), pltpu.VMEM((LANES, L), F32)],
        compiler_params=_cparams("parallel", "arbitrary"),
        name="ssd_chunk_scan",
    )(zx, zx, zx, zx, zx, zx, zx, dt, alog, cw, cb, cw, cb, cw, cb, dskip, norm_g)


def _attn_kernel(qi_ref, kj_ref, q_ref, k_ref, v_ref, lq1_ref, lk1_ref, lq2_ref, lk2_ref, sg_ref,
                 o_ref, m_ref, l_ref, acc_ref, *, lambda_init):
    s_idx = pl.program_id(1)
    qi = qi_ref[s_idx]
    kj = kj_ref[s_idx]
    T = ATT_T
    HD = DA_HEAD_DIM

    @pl.when(kj == 0)
    def _():
        m_ref[...] = jnp.full_like(m_ref, -jnp.inf)
        l_ref[...] = jnp.zeros_like(l_ref)
        acc_ref[...] = jnp.zeros_like(acc_ref)

    def step(masked):
        v = v_ref[...]
        if masked:
            row = lax.broadcasted_iota(jnp.int32, (T, T), 0)
            col = lax.broadcasted_iota(jnp.int32, (T, T), 1)
            keep = col <= row
        for t in range(2):
            kt = k_ref[:, t * HD:(t + 1) * HD]
            for gq in range(DA_GROUP):
                idx = gq * 2 + t
                q = q_ref[:, idx * HD:(idx + 1) * HD]
                s = lax.dot_general(q, kt, (((1,), (1,)), ((), ())), preferred_element_type=F32)
                if masked:
                    s = jnp.where(keep, s, -jnp.inf)
                m_prev = m_ref[idx]
                m_new = jnp.maximum(m_prev, jnp.max(s, axis=-1, keepdims=True))
                alpha = jnp.exp2(m_prev - m_new)
                p = jnp.exp2(s - _lane_tile(m_new, T // LANES))
                psum = p[:, 0:LANES]
                for b in range(1, T // LANES):
                    psum = psum + p[:, b * LANES:(b + 1) * LANES]
                l_ref[idx] = alpha * l_ref[idx] + psum
                acc_ref[idx] = _lane_tile(alpha, DA_V_DIM // LANES) * acc_ref[idx] + _dot(p.astype(BF16), v)
                m_ref[idx] = m_new

    @pl.when(kj < qi)
    def _():
        step(False)

    @pl.when(kj == qi)
    def _():
        step(True)
        lam = (jnp.exp(jnp.sum(lq1_ref[...] * lk1_ref[...], axis=-1, keepdims=True))
               - jnp.exp(jnp.sum(lq2_ref[...] * lk2_ref[...], axis=-1, keepdims=True)) + lambda_init)
        for gq in range(DA_GROUP):
            a0 = acc_ref[gq * 2] / jnp.sum(l_ref[gq * 2], axis=-1, keepdims=True)
            a1 = acc_ref[gq * 2 + 1] / jnp.sum(l_ref[gq * 2 + 1], axis=-1, keepdims=True)
            o = a0 - lam * a1
            ms = jnp.mean(o * o, axis=-1, keepdims=True)
            o = (o * lax.rsqrt(ms + EPS) * sg_ref[...]) * (1.0 - lambda_init)
            o_ref[:, gq * DA_V_DIM:(gq + 1) * DA_V_DIM] = o.astype(o_ref.dtype)


def _attn_call(q, kv, lq1, lk1, lq2, lk2, subln_g, lambda_init):
    S = q.shape[0]
    v_off = K_DIM // DA_V_DIM
    T = ATT_T
    nq = S // T
    pairs = [(i, j) for i in range(nq) for j in range(i + 1)]
    qi_tab = jnp.asarray(np.array([p[0] for p in pairs], np.int32))
    kj_tab = jnp.asarray(np.array([p[1] for p in pairs], np.int32))
    qw = DA_GROUP * 2 * DA_HEAD_DIM
    kw = 2 * DA_HEAD_DIM
    vec = pl.BlockSpec((1, DA_HEAD_DIM), lambda h, s, qt, kt: (0, 0))
    grid_spec = pltpu.PrefetchScalarGridSpec(
        num_scalar_prefetch=2,
        grid=(DA_KV_HEADS, len(pairs)),
        in_specs=[pl.BlockSpec((T, qw), lambda h, s, qt, kt: (qt[s], h)),
                  pl.BlockSpec((T, kw), lambda h, s, qt, kt: (kt[s], h)),
                  pl.BlockSpec((T, DA_V_DIM), lambda h, s, qt, kt: (kt[s], v_off + h)),
                  vec, vec, vec, vec,
                  pl.BlockSpec((1, DA_V_DIM), lambda h, s, qt, kt: (0, 0))],
        out_specs=pl.BlockSpec((T, DA_GROUP * DA_V_DIM), lambda h, s, qt, kt: (qt[s], h)),
        scratch_shapes=[pltpu.VMEM((2 * DA_GROUP, T, LANES), F32),
                        pltpu.VMEM((2 * DA_GROUP, T, LANES), F32),
                        pltpu.VMEM((2 * DA_GROUP, T, DA_V_DIM), F32)],
    )
    return pl.pallas_call(
        functools.partial(_attn_kernel, lambda_init=lambda_init),
        grid_spec=grid_spec,
        out_shape=jax.ShapeDtypeStruct((S, DA_HEADS * DA_V_DIM), BF16),
        compiler_params=_cparams("parallel", "arbitrary"),
        name="diff_attention",
    )(qi_tab, kj_tab, q, kv, kv, lq1, lk1, lq2, lk2, subln_g)


def _router_kernel(x_ref, g_ref, sh_ref, sc_ref, rw_ref, rb_ref, h_ref, idx_ref, wt_ref, *, n_real):
    i = pl.program_id(0)

    @pl.when(i >= n_real)
    def _():
        h_ref[...] = jnp.zeros_like(h_ref)

    @pl.when(i < n_real)
    def _():
        _route(x_ref, g_ref, sh_ref, sc_ref, rw_ref, rb_ref, h_ref, idx_ref, wt_ref)


def _route(x_ref, g_ref, sh_ref, sc_ref, rw_ref, rb_ref, h_ref, idx_ref, wt_ref):
    h = _norm_mod(x_ref[...], g_ref[...], sh_ref[...], sc_ref[...])
    h_ref[...] = h.astype(BF16)
    hh = h.astype(BF16)
    hl = (h - hh.astype(F32)).astype(BF16)
    rw = rw_ref[...]
    wh = rw.astype(BF16)
    wl = (rw - wh.astype(F32)).astype(BF16)
    logits = _dot(hh, wh) + _dot(hl, wh) + _dot(hh, wl)
    lt = logits.T[0:N_EXPERTS, :]
    score = _sigmoid(lt)
    sel = score + rb_ref[...]
    srow = [score[e:e + 1, :] for e in range(N_EXPERTS)]
    row = [sel[e:e + 1, :] for e in range(N_EXPERTS)]

    gscore = []
    for gi in range(N_EXPERT_GROUPS):
        r = row[gi * EXPERTS_PER_GROUP:(gi + 1) * EXPERTS_PER_GROUP]
        best = None
        for a in range(EXPERTS_PER_GROUP):
            for b in range(a + 1, EXPERTS_PER_GROUP):
                s = r[a] + r[b]
                best = s if best is None else jnp.maximum(best, s)
        gscore.append(best)
    bestg = jnp.zeros_like(gscore[0], dtype=jnp.int32)
    bestv = gscore[0]
    for gi in range(1, N_EXPERT_GROUPS):
        upd = gscore[gi] > bestv
        bestg = jnp.where(upd, gi, bestg)
        bestv = jnp.where(upd, gscore[gi], bestv)

    cand = [jnp.where(bestg == e // EXPERTS_PER_GROUP, row[e], -jnp.inf) for e in range(N_EXPERTS)]

    def first_argmax(vals):
        bi = jnp.zeros_like(bestg)
        bv = vals[0]
        for e in range(1, N_EXPERTS):
            upd = vals[e] > bv
            bi = jnp.where(upd, e, bi)
            bv = jnp.where(upd, vals[e], bv)
        return bi

    i1 = first_argmax(cand)
    i2 = first_argmax([jnp.where(i1 == e, -jnp.inf, cand[e]) for e in range(N_EXPERTS)])
    s1 = sum(jnp.where(i1 == e, srow[e], 0.0) for e in range(N_EXPERTS))
    s2 = sum(jnp.where(i2 == e, srow[e], 0.0) for e in range(N_EXPERTS))
    den = s1 + s2
    tm = i1.shape[1]
    idx_ref[...] = jnp.concatenate([i1, i2, jnp.zeros((6, tm), jnp.int32)], axis=0)
    wt_ref[...] = jnp.concatenate([s1 / den, s2 / den, jnp.zeros((LANES - 2, tm), F32)], axis=0).T


def _router_call(x, g, sh, sc, rw_pad, rb_col, h_rows):
    S, D = x.shape
    tm = ROUTER_TM
    n_real = S // tm
    last = n_real - 1
    vec = pl.BlockSpec((1, D), lambda i: (0, 0))
    return pl.pallas_call(
        functools.partial(_router_kernel, n_real=n_real),
        grid=(h_rows // tm,),
        in_specs=[pl.BlockSpec((tm, D), lambda i: (jnp.minimum(i, last), 0)), vec, vec, vec,
                  pl.BlockSpec((D, LANES), lambda i: (0, 0)),
                  pl.BlockSpec((N_EXPERTS, 1), lambda i: (0, 0))],
        out_specs=[pl.BlockSpec((tm, D), lambda i: (i, 0)),
                   pl.BlockSpec((8, tm), lambda i: (0, jnp.minimum(i, last))),
                   pl.BlockSpec((tm, LANES), lambda i: (jnp.minimum(i, last), 0))],
        out_shape=[jax.ShapeDtypeStruct((h_rows, D), BF16),
                   jax.ShapeDtypeStruct((8, S), jnp.int32),
                   jax.ShapeDtypeStruct((S, LANES), F32)],
        compiler_params=_cparams("arbitrary"),
        name="moe_norm_router",
    )(x, g, sh, sc, rw_pad, rb_col)


def _moe_kernel(ti_ref, te_ref, tf_ref, nv_ref, x_ref, wg_ref, wu_ref, wd_ref, o_ref,
                wgb_ref, wub_ref, wdb_ref):
    t = pl.program_id(0)

    @pl.when(t < nv_ref[0])
    def _():
        @pl.when(tf_ref[t] == 1)
        def _():
            wgb_ref[...] = wg_ref[...].astype(BF16)
            wub_ref[...] = wu_ref[...].astype(BF16)
            wdb_ref[...] = wd_ref[...].astype(BF16)

        x = x_ref[...]
        gt = _dot(x, wgb_ref[...])
        up = _dot(x, wub_ref[...])
        h = (gt * _sigmoid(gt)) * up
        o_ref[...] = _dot(h.astype(BF16), wdb_ref[...]).astype(o_ref.dtype)


def _moe_call(xs, tile_idx, tile_expert, tile_first, n_valid, w_gate, w_up, w_down, layer):
    R, D = xs.shape
    tm = MOE_TM
    F = D_EXPERT
    grid_spec = pltpu.PrefetchScalarGridSpec(
        num_scalar_prefetch=4,
        grid=(R // tm,),
        in_specs=[pl.BlockSpec((tm, D), lambda t, ti, te, tf, nv: (ti[t], 0)),
                  pl.BlockSpec((None, None, D, F), lambda t, ti, te, tf, nv: (layer, te[t], 0, 0)),
                  pl.BlockSpec((None, None, D, F), lambda t, ti, te, tf, nv: (layer, te[t], 0, 0)),
                  pl.BlockSpec((None, None, F, D), lambda t, ti, te, tf, nv: (layer, te[t], 0, 0))],
        out_specs=pl.BlockSpec((tm, D), lambda t, ti, te, tf, nv: (ti[t], 0)),
        scratch_shapes=[pltpu.VMEM((D, F), BF16), pltpu.VMEM((D, F), BF16), pltpu.VMEM((F, D), BF16)],
    )
    return pl.pallas_call(
        _moe_kernel,
        grid_spec=grid_spec,
        out_shape=jax.ShapeDtypeStruct((R, D), BF16),
        compiler_params=_cparams("arbitrary"),
        name="moe_experts",
    )(tile_idx, tile_expert, tile_first, n_valid, xs, w_gate, w_up, w_down)


def _combine_kernel(x_ref, y1_ref, y2_ref, wt_ref, g_ref, fg_ref, o_ref, *, final):
    wt = wt_ref[...]
    y = wt[:, 0:1] * y1_ref[...].astype(F32) + wt[:, 1:2] * y2_ref[...].astype(F32)
    x = x_ref[...] + g_ref[...] * y
    if final:
        ms = jnp.mean(x * x, axis=-1, keepdims=True)
        x = x * lax.rsqrt(ms + EPS) * fg_ref[...]
    o_ref[...] = x


def _combine_call(x, y1, y2, wts, gate, final_g, final):
    S, D = x.shape
    tm = COMBINE_TM
    blk = pl.BlockSpec((tm, D), lambda i: (i, 0))
    vec = pl.BlockSpec((1, D), lambda i: (0, 0))
    return pl.pallas_call(
        functools.partial(_combine_kernel, final=final),
        grid=(S // tm,),
        in_specs=[blk, blk, blk, pl.BlockSpec((tm, LANES), lambda i: (i, 0)), vec, vec],
        out_specs=blk,
        out_shape=jax.ShapeDtypeStruct((S, D), F32),
        compiler_params=_cparams("parallel"),
        name="moe_combine",
    )(x, y1, y2, wts, gate, final_g)


def _moe_rows(S):
    return TOP_K * S + N_EXPERTS * MOE_TM


def _dispatch_plan(idx, S):
    tm = MOE_TM
    R = _moe_rows(S)
    n_tiles = R // tm
    e_flat = idx[:TOP_K].reshape(-1)
    tok_flat = jnp.tile(jnp.arange(S, dtype=jnp.int32), TOP_K)
    onehot = (e_flat[:, None] == jnp.arange(N_EXPERTS, dtype=jnp.int32)[None, :]).astype(jnp.int32)
    incl = jnp.cumsum(onehot, axis=0)
    counts = incl[-1]
    tiles_e = (counts + tm - 1) // tm
    tile_end = jnp.cumsum(tiles_e)
    tile_start = tile_end - tiles_e
    dest = jnp.sum(onehot * (incl - 1 + (tile_start * tm)[None, :]), axis=1)
    src_tok = jnp.zeros((R,), jnp.int32).at[dest].set(tok_flat)
    n_valid = tile_end[-1]
    t_ids = jnp.minimum(jnp.arange(n_tiles, dtype=jnp.int32), n_valid - 1)
    tile_expert = jnp.sum((t_ids[:, None] >= tile_end[None, :]).astype(jnp.int32), axis=1)
    tile_first = jnp.sum(((t_ids[:, None] == tile_start[None, :]) & (tiles_e[None, :] > 0)).astype(jnp.int32),
                         axis=1)
    return (src_tok, t_ids, tile_expert, tile_first, n_valid.reshape(1).astype(jnp.int32), dest[:S], dest[S:])


def _moe_block(x, g, sh, sc, gate, rw_pad, rb_col, w_gate, w_up, w_down, layer, final_g, final):
    S = x.shape[0]
    h, idx, wts = _router_call(x, g, sh, sc, rw_pad, rb_col, _moe_rows(S) + ROUTER_TM)
    src_tok, t_ids, t_exp, t_first, n_valid, pos1, pos2 = _dispatch_plan(idx, S)
    xs = jnp.take(h, src_tok, axis=0)
    ys = _moe_call(xs, t_ids, t_exp, t_first, n_valid, w_gate, w_up, w_down, layer)
    y1 = jnp.take(ys, pos1, axis=0)
    y2 = jnp.take(ys, pos2, axis=0)
    return _combine_call(x, y1, y2, wts, gate, final_g, final)


def _rope_tables(S):
    inv = 1.0 / (ROPE_THETA ** (jnp.arange(0, ROPE_DIM, 2, dtype=F32) / ROPE_DIM))
    ang = jnp.arange(S, dtype=F32)[:, None] * inv[None, :]
    cos, sin = jnp.cos(ang), jnp.sin(ang)
    zeros = jnp.zeros((S, LANES - ROPE_DIM), F32)
    zh = jnp.zeros((S, ROPE_HALF), F32)
    cos_t = jnp.concatenate([cos, cos, jnp.ones((S, LANES - ROPE_DIM), F32)], axis=1)
    sin_a = jnp.concatenate([-sin, zh, zeros], axis=1)
    sin_b = jnp.concatenate([zh, sin, zeros], axis=1)
    return cos_t, sin_a, sin_b


def kernel(x, c, mod_w, mod_b, norm1_g, norm2_g, ssm_in_w, ssm_conv_w, ssm_conv_b, ssm_dt_bias, ssm_a_log, ssm_d, ssm_norm_g, ssm_out_w, kv_mod_w, kv_mod_b, kv_norm_g, w_kv, attn_q_w, lam_q1, lam_k1, lam_q2, lam_k2, subln_g, attn_o_w, moe_w_gate, moe_w_up, moe_w_down, router_w, router_b, final_g):
    _, S, D = x.shape
    xr = x[0]
    cb = jnp.broadcast_to(c[0][:, None], (D, LANES))
    mods = _mod_call(cb, mod_w, mod_b)
    kvm = _mod_call(cb, kv_mod_w[None], kv_mod_b[None])
    cos_t, sin_a, sin_b = _rope_tables(S)
    rw_pad = jnp.pad(router_w, ((0, 0), (0, LANES - N_EXPERTS)))
    rb_col = router_b.reshape(N_EXPERTS, 1)
    row = lambda v: v.reshape(1, -1)
    pad_l = lambda v: jnp.pad(v.reshape(1, -1), ((0, 0), (0, LANES - v.shape[-1])))

    kv = None
    for layer in range(DEPTH):
        m = mods[layer]
        sh1, sc1, g1, sh2, sc2, g2 = [m[:, i * D:(i + 1) * D] for i in range(N_MOD)]
        if layer == N_A_LAYERS:
            kv_sh, kv_sc = kvm[0][:, :D], kvm[0][:, D:]
            kv = _rope_proj_call(xr, row(kv_norm_g), kv_sh, kv_sc, w_kv.astype(BF16), cos_t, sin_a, sin_b,
                                 rope_cols=K_DIM, scale=1.0, name="kv_proj")
        if layer < N_A_LAYERS:
            i = layer
            w_in = ssm_in_w[i]
            zx, dt = _in_proj_call(xr, row(norm1_g[layer]), sh1, sc1,
                                   w_in[:, :ZX_DIM].astype(BF16),
                                   jnp.pad(w_in[:, ZX_DIM:], ((0, 0), (0, LANES - SSM_HEADS))).astype(BF16),
                                   pad_l(ssm_dt_bias[i]))
            y = _ssd_call(zx, dt, pad_l(ssm_a_log[i]), ssm_conv_w[i], ssm_conv_b[i],
                          row(jnp.repeat(ssm_d[i], SSM_HEAD_DIM)), row(ssm_norm_g[i]))
            xr = _res_mm_call(y, ssm_out_w[i].astype(BF16), xr, g1, name="ssm_out_proj")
        else:
            j = layer - N_A_LAYERS
            lambda_init = 0.8 - 0.6 * float(np.exp(-0.3 * layer))
            q = _rope_proj_call(xr, row(norm1_g[layer]), sh1, sc1, attn_q_w[j].astype(BF16),
                                cos_t, sin_a, sin_b, rope_cols=D, scale=DA_HEAD_DIM ** -0.5 * LOG2E,
                                name="q_proj")
            o = _attn_call(q, kv, row(lam_q1[j]), row(lam_k1[j]), row(lam_q2[j]), row(lam_k2[j]),
                           row(subln_g[j]), lambda_init)
            xr = _res_mm_call(o, attn_o_w[j].astype(BF16), xr, g1, name="attn_o_proj")
        xr = _moe_block(xr, row(norm2_g[layer]), sh2, sc2, g2, rw_pad, rb_col,
                        moe_w_gate, moe_w_up, moe_w_down, layer, row(final_g), layer == DEPTH - 1)
    return xr[None]
```

```python
import functools

import numpy as np
import jax
import jax.numpy as jnp
from jax import lax
from jax.experimental import pallas as pl
from jax.experimental.pallas import tpu as pltpu

F32 = jnp.float32
BF16 = jnp.bfloat16

D_MODEL = 2048
DEPTH = 4
N_A_LAYERS = DEPTH // 2
EPS = 1e-5
N_MOD = 6

D_INNER = 2 * D_MODEL
SSM_HEAD_DIM = 64
SSM_HEADS = D_INNER // SSM_HEAD_DIM
SSM_GROUPS = 8
SSM_HPG = SSM_HEADS // SSM_GROUPS
SSM_STATE = 128
CONV_WIDTH = 4
CHUNK = 256
GROUP_W = SSM_HPG * SSM_HEAD_DIM
ZX_DIM = 2 * D_INNER + 2 * SSM_GROUPS * SSM_STATE

DA_HEADS = 8
DA_KV_HEADS = 4
DA_GROUP = DA_HEADS // DA_KV_HEADS
DA_HEAD_DIM = D_MODEL // DA_HEADS // 2
DA_V_DIM = 2 * DA_HEAD_DIM
K_DIM = DA_KV_HEADS * 2 * DA_HEAD_DIM
V_DIM = DA_KV_HEADS * DA_V_DIM
ROPE_DIM = DA_HEAD_DIM // 4
ROPE_HALF = ROPE_DIM // 2
ROPE_THETA = 500000.0

N_EXPERTS = 16
N_EXPERT_GROUPS = 4
EXPERTS_PER_GROUP = N_EXPERTS // N_EXPERT_GROUPS
TOP_K = 2
D_EXPERT = 512

LOG2E = 1.4426950408889634
LANES = 128
HALO = 16
VMEM_LIMIT = 48 * 1024 * 1024

PROJ_TM = 1024
PROJ_TN = 512
ATT_T = 512
MOE_TM = 256
ROUTER_TM = 512
COMBINE_TM = 512


def _cparams(*sem):
    return pltpu.CompilerParams(dimension_semantics=sem, vmem_limit_bytes=VMEM_LIMIT)


def _sigmoid(v):
    return 1.0 / (1.0 + jnp.exp(-v))


def _lane_tile(v, n):
    return v if n == 1 else jnp.concatenate([v] * n, axis=1)


def _norm_mod(x, g, sh, sc):
    ms = jnp.mean(x * x, axis=-1, keepdims=True)
    return (x * lax.rsqrt(ms + EPS) * g) * (1.0 + sc) + sh


def _split3(v):
    hi = v.astype(BF16)
    r1 = v - hi.astype(F32)
    mid = r1.astype(BF16)
    lo = (r1 - mid.astype(F32)).astype(BF16)
    return hi, mid, lo


def _dot(a, b):
    return jnp.dot(a, b, preferred_element_type=F32)


def _dot_parts(parts, sel):
    acc = _dot(parts[0], sel)
    for p in parts[1:]:
        acc = acc + _dot(p, sel)
    return acc


def _dot3_l(sel, v):
    hi, mid, lo = _split3(v)
    return _dot(sel, hi) + _dot(sel, mid) + _dot(sel, lo)


def _mod_kernel(c_ref, w_ref, b_ref, o_ref):
    cc = c_ref[...]
    sc = cc * _sigmoid(cc)
    w = w_ref[...]
    o_ref[...] = jnp.sum(w * _lane_tile(sc, w.shape[1] // LANES), axis=0, keepdims=True) + b_ref[...]


def _mod_call(cb, w, b):
    L, D, N = w.shape
    tn = 1024
    return pl.pallas_call(
        _mod_kernel,
        grid=(L, N // tn),
        in_specs=[pl.BlockSpec((D, LANES), lambda l, j: (0, 0)),
                  pl.BlockSpec((None, D, tn), lambda l, j: (l, 0, j)),
                  pl.BlockSpec((None, 1, tn), lambda l, j: (l, 0, j))],
        out_specs=pl.BlockSpec((None, 1, tn), lambda l, j: (l, 0, j)),
        out_shape=jax.ShapeDtypeStruct((L, 1, N), F32),
        compiler_params=_cparams("parallel", "parallel"),
        name="mod_vectors",
    )(cb, w, b.reshape(L, 1, N))


def _norm_prologue(x_ref, g_ref, sh_ref, sc_ref, h_ref):
    tm = x_ref.shape[0]
    rc = 256
    for r in range(0, tm, rc):
        h = _norm_mod(x_ref[r:r + rc, :], g_ref[...], sh_ref[...], sc_ref[...])
        h_ref[r:r + rc, :] = h.astype(BF16)


def _in_proj_kernel(x_ref, g_ref, sh_ref, sc_ref, w_ref, wdt_ref, dtb_ref, o_ref, dt_ref, h_ref):
    @pl.when(pl.program_id(1) == 0)
    def _():
        _norm_prologue(x_ref, g_ref, sh_ref, sc_ref, h_ref)
        d = _dot(h_ref[...], wdt_ref[...]) + dtb_ref[...]
        dt_ref[...] = jnp.maximum(d, 0.0) + jnp.log1p(jnp.exp(-jnp.abs(d)))

    o_ref[...] = _dot(h_ref[...], w_ref[...]).astype(o_ref.dtype)


def _in_proj_call(x, g, sh, sc, w, wdt, dtb):
    M, K = x.shape
    N = ZX_DIM
    tm, tn = PROJ_TM, PROJ_TN
    vec = pl.BlockSpec((1, K), lambda i, j: (0, 0))
    return pl.pallas_call(
        _in_proj_kernel,
        grid=(M // tm, N // tn),
        in_specs=[pl.BlockSpec((tm, K), lambda i, j: (i, 0)), vec, vec, vec,
                  pl.BlockSpec((K, tn), lambda i, j: (0, j)),
                  pl.BlockSpec((K, LANES), lambda i, j: (0, 0)),
                  pl.BlockSpec((1, LANES), lambda i, j: (0, 0))],
        out_specs=[pl.BlockSpec((tm, tn), lambda i, j: (i, j)),
                   pl.BlockSpec((tm, LANES), lambda i, j: (i, 0))],
        out_shape=[jax.ShapeDtypeStruct((M, N), BF16), jax.ShapeDtypeStruct((M, LANES), F32)],
        scratch_shapes=[pltpu.VMEM((tm, K), BF16)],
        compiler_params=_cparams("parallel", "arbitrary"),
        name="ssm_in_proj",
    )(x, g, sh, sc, w, wdt, dtb)


def _rope_proj_kernel(x_ref, g_ref, sh_ref, sc_ref, w_ref, cos_ref, sa_ref, sb_ref, o_ref, h_ref,
                      *, rope_tiles, n_tiles, scale):
    j = pl.program_id(1)

    @pl.when(j == 0)
    def _():
        _norm_prologue(x_ref, g_ref, sh_ref, sc_ref, h_ref)

    acc = _dot(h_ref[...], w_ref[...])
    tn = acc.shape[1]
    nb = tn // LANES

    def roped():
        cos = _lane_tile(cos_ref[...], nb)
        sa = _lane_tile(sa_ref[...], nb)
        sb = _lane_tile(sb_ref[...], nb)
        r = acc * cos + pltpu.roll(acc, tn - ROPE_HALF, 1) * sa + pltpu.roll(acc, ROPE_HALF, 1) * sb
        return (r * scale).astype(o_ref.dtype)

    if rope_tiles >= n_tiles:
        o_ref[...] = roped()
    else:
        @pl.when(j < rope_tiles)
        def _():
            o_ref[...] = roped()

        @pl.when(j >= rope_tiles)
        def _():
            o_ref[...] = acc.astype(o_ref.dtype)


def _rope_proj_call(x, g, sh, sc, w, cos, sa, sb, *, rope_cols, scale, name):
    M, K = x.shape
    N = w.shape[1]
    tm, tn = PROJ_TM, PROJ_TN
    vec = pl.BlockSpec((1, K), lambda i, j: (0, 0))
    tab = pl.BlockSpec((tm, LANES), lambda i, j: (i, 0))
    kern = functools.partial(_rope_proj_kernel, rope_tiles=rope_cols // tn, n_tiles=N // tn, scale=scale)
    return pl.pallas_call(
        kern,
        grid=(M // tm, N // tn),
        in_specs=[pl.BlockSpec((tm, K), lambda i, j: (i, 0)), vec, vec, vec,
                  pl.BlockSpec((K, tn), lambda i, j: (0, j)), tab, tab, tab],
        out_specs=pl.BlockSpec((tm, tn), lambda i, j: (i, j)),
        out_shape=jax.ShapeDtypeStruct((M, N), BF16),
        scratch_shapes=[pltpu.VMEM((tm, K), BF16)],
        compiler_params=_cparams("parallel", "arbitrary"),
        name=name,
    )(x, g, sh, sc, w, cos, sa, sb)


def _res_mm_kernel(a_ref, w_ref, res_ref, gate_ref, o_ref):
    o_ref[...] = res_ref[...] + gate_ref[...] * _dot(a_ref[...], w_ref[...])


def _res_mm_call(a, w, res, gate, *, name):
    M, K = a.shape
    N = w.shape[1]
    tm, tn = PROJ_TM, PROJ_TN
    return pl.pallas_call(
        _res_mm_kernel,
        grid=(M // tm, N // tn),
        in_specs=[pl.BlockSpec((tm, K), lambda i, j: (i, 0)),
                  pl.BlockSpec((K, tn), lambda i, j: (0, j)),
                  pl.BlockSpec((tm, tn), lambda i, j: (i, j)),
                  pl.BlockSpec((1, tn), lambda i, j: (0, j))],
        out_specs=pl.BlockSpec((tm, tn), lambda i, j: (i, j)),
        out_shape=jax.ShapeDtypeStruct((M, N), F32),
        compiler_params=_cparams("parallel", "parallel"),
        name=name,
    )(a, w, res, gate)


def _ssd_kernel(z_ref, x_ref, xh_ref, b_ref, bh_ref, c_ref, ch_ref, dt_ref, alog_ref,
                cwx_ref, cbx_ref, cwb_ref, cbb_ref, cwc_ref, cbc_ref, dsk_ref, ng_ref,
                o_ref, h_ref, cst_ref, shift_ref, sel_ref, esel_ref):
    g = pl.program_id(0)
    c = pl.program_id(1)
    L = CHUNK
    row = lax.broadcasted_iota(jnp.int32, (L, L), 0)
    col = lax.broadcasted_iota(jnp.int32, (L, L), 1)
    causal = row >= col

    @pl.when(c == 0)
    def _():
        h_ref[...] = jnp.zeros_like(h_ref)
        shift_ref[0] = jnp.where(causal, 1.0, 0.0).astype(BF16)
        for k in range(1, CONV_WIDTH):
            shift_ref[k] = jnp.where(row - col == k, 1.0, 0.0).astype(BF16)
        k1 = lax.broadcasted_iota(jnp.int32, (LANES, LANES), 0)
        n1 = lax.broadcasted_iota(jnp.int32, (LANES, LANES), 1)
        sel_ref[...] = jnp.where(jnp.where(n1 < SSM_HPG, k1 - n1, -1) == g * SSM_HPG, 1.0, 0.0).astype(BF16)
        k2 = lax.broadcasted_iota(jnp.int32, (LANES, GROUP_W), 0)
        n2 = lax.broadcasted_iota(jnp.int32, (LANES, GROUP_W), 1) // SSM_HEAD_DIM
        esel_ref[...] = jnp.where(k2 - n2 == g * SSM_HPG, 1.0, 0.0).astype(BF16)

    def conv_silu(cur_ref, halo_ref, w_ref, bias_ref):
        cur_b = cur_ref[...]
        w = w_ref[...]
        acc = bias_ref[...] + w[CONV_WIDTH - 1:CONV_WIDTH, :] * cur_b.astype(F32)
        prev8 = jnp.where(c == 0, 0.0, halo_ref[...].astype(F32))[HALO - 8:HALO, :]
        row8 = lax.broadcasted_iota(jnp.int32, prev8.shape, 0)
        top = jnp.zeros_like(prev8)
        for k in range(1, CONV_WIDTH):
            wk = w[CONV_WIDTH - 1 - k:CONV_WIDTH - k, :]
            acc = acc + wk * _dot(shift_ref[k], cur_b)
            top = top + wk * jnp.where(row8 < k, pltpu.roll(prev8, k, 0), 0.0)
        acc = jnp.concatenate([acc[0:8, :] + top, acc[8:, :]], axis=0)
        return acc * _sigmoid(acc)

    xs = conv_silu(x_ref, xh_ref, cwx_ref, cbx_ref)
    bm = conv_silu(b_ref, bh_ref, cwb_ref, cbb_ref)
    cm = conv_silu(c_ref, ch_ref, cwc_ref, cbc_ref)

    dt_all = dt_ref[...]
    dta = dt_all * (-jnp.exp(alog_ref[...]))
    cs_all = _dot3_l(shift_ref[0], dta)
    cst_ref[...] = cs_all.T
    cs_r = cst_ref[pl.ds(pl.multiple_of(g * SSM_HPG, SSM_HPG), SSM_HPG), :]

    sel = sel_ref[...]
    esel = esel_ref[...]
    cs_parts = _split3(cs_all)
    cs_g = _dot_parts(cs_parts, sel)
    cs_x = _dot_parts(cs_parts, esel)
    dt_x = _dot_parts(_split3(dt_all)[:2], esel)

    xdt = xs * dt_x
    xb = xdt.astype(BF16)
    bb = bm.astype(BF16)
    cb = cm.astype(BF16)
    cbm = lax.dot_general(cb, bb, (((1,), (1,)), ((), ())), preferred_element_type=F32)

    half = GROUP_W // 2
    lane_head = lax.broadcasted_iota(jnp.int32, (L, half), 1) // SSM_HEAD_DIM
    yd = []
    for blk in range(2):
        xblk = xb[:, blk * half:(blk + 1) * half]
        acc = jnp.zeros((L, half), F32)
        for e4 in range(SSM_HPG // 2):
            e = blk * (SSM_HPG // 2) + e4
            diff = cs_g[:, e:e + 1] - cs_r[e:e + 1, :]
            lm = jnp.exp(jnp.where(causal, diff, -jnp.inf))
            m = (cbm * lm).astype(BF16)
            xm = jnp.where(lane_head == e4, xblk, jnp.zeros_like(xblk))
            acc = acc + _dot(m, xm)
        yd.append(acc)
    y = jnp.concatenate(yd, axis=1)

    hprev = h_ref[...]
    y = y + _dot(cb, hprev.astype(BF16)) * jnp.exp(cs_x)
    cs_last = cs_x[L - 1:L, :]
    xdec = (xdt * jnp.exp(cs_last - cs_x)).astype(BF16)
    states = lax.dot_general(bb, xdec, (((0,), (0,)), ((), ())), preferred_element_type=F32)
    h_ref[...] = hprev * jnp.exp(cs_last) + states

    y = y + xs * dsk_ref[...]
    zz = z_ref[...].astype(F32)
    y = y * (zz * _sigmoid(zz))
    ms = jnp.mean(y * y, axis=-1, keepdims=True)
    o_ref[...] = (y * lax.rsqrt(ms + EPS) * ng_ref[...]).astype(o_ref.dtype)


def _ssd_call(zx, dt, alog, conv_w, conv_b, dskip, norm_g):
    S = zx.shape[0]
    G, L, W, N = SSM_GROUPS, CHUNK, GROUP_W, SSM_STATE
    nc = S // L
    xo = D_INNER // W
    bo = 2 * D_INNER // N
    co = bo + G
    hb = L // HALO

    def halo(off):
        return lambda g, c: (jnp.maximum(c * hb - 1, 0), off + g)

    cw = conv_w
    cb = conv_b.reshape(1, -1)
    cxo, cbo, cco = 0, D_INNER // N, D_INNER // N + G
    in_specs = [
        pl.BlockSpec((L, W), lambda g, c: (c, g)),
        pl.BlockSpec((L, W), lambda g, c: (c, xo + g)),
        pl.BlockSpec((HALO, W), halo(xo)),
        pl.BlockSpec((L, N), lambda g, c: (c, bo + g)),
        pl.BlockSpec((HALO, N), halo(bo)),
        pl.BlockSpec((L, N), lambda g, c: (c, co + g)),
        pl.BlockSpec((HALO, N), halo(co)),
        pl.BlockSpec((L, LANES), lambda g, c: (c, 0)),
        pl.BlockSpec((1, LANES), lambda g, c: (0, 0)),
        pl.BlockSpec((CONV_WIDTH, W), lambda g, c: (0, cxo + g)),
        pl.BlockSpec((1, W), lambda g, c: (0, cxo + g)),
        pl.BlockSpec((CONV_WIDTH, N), lambda g, c: (0, cbo + g)),
        pl.BlockSpec((1, N), lambda g, c: (0, cbo + g)),
        pl.BlockSpec((CONV_WIDTH, N), lambda g, c: (0, cco + g)),
        pl.BlockSpec((1, N), lambda g, c: (0, cco + g)),
        pl.BlockSpec((1, W), lambda g, c: (0, g)),
        pl.BlockSpec((1, W), lambda g, c: (0, g)),
    ]
    return pl.pallas_call(
        _ssd_kernel,
        grid=(G, nc),
        in_specs=in_specs,
        out_specs=pl.BlockSpec((L, W), lambda g, c: (c, g)),
        out_shape=jax.ShapeDtypeStruct((S, D_INNER), BF16),
        scratch_shapes=[pltpu.VMEM((N, W), F32), pltpu.VMEM((LANES, L), F32),
                        pltpu.VMEM((CONV_WIDTH, L, L), BF16), pltpu.VMEM((LANES, LANES), BF16),
                        pltpu.VMEM((LANES, W), BF16)],
        compiler_params=_cparams("parallel", "arbitrary"),
        name="ssd_chunk_scan",
    )(zx, zx, zx, zx, zx, zx, zx, dt, alog, cw, cb, cw, cb, cw, cb, dskip, norm_g)


def _attn_kernel(qi_ref, kj_ref, q_ref, k_ref, v_ref, lq1_ref, lk1_ref, lq2_ref, lk2_ref, sg_ref,
                 o_ref, m_ref, l_ref, acc_ref, *, lambda_init):
    s_idx = pl.program_id(1)
    qi = qi_ref[s_idx]
    kj = kj_ref[s_idx]
    T = ATT_T
    HD = DA_HEAD_DIM

    @pl.when(kj == 0)
    def _():
        m_ref[...] = jnp.full_like(m_ref, -jnp.inf)
        l_ref[...] = jnp.zeros_like(l_ref)
        acc_ref[...] = jnp.zeros_like(acc_ref)

    def step(masked):
        v = v_ref[...]
        if masked:
            row = lax.broadcasted_iota(jnp.int32, (T, T), 0)
            col = lax.broadcasted_iota(jnp.int32, (T, T), 1)
            keep = col <= row
        for t in range(2):
            kt = k_ref[:, t * HD:(t + 1) * HD]
            for gq in range(DA_GROUP):
                idx = gq * 2 + t
                q = q_ref[:, idx * HD:(idx + 1) * HD]
                s = lax.dot_general(q, kt, (((1,), (1,)), ((), ())), preferred_element_type=F32)
                if masked:
                    s = jnp.where(keep, s, -jnp.inf)
                m_prev = m_ref[idx]
                m_new = jnp.maximum(m_prev, jnp.max(s, axis=-1, keepdims=True))
                alpha = jnp.exp2(m_prev - m_new)
                p = jnp.exp2(s - _lane_tile(m_new, T // LANES))
                psum = p[:, 0:LANES]
                for b in range(1, T // LANES):
                    psum = psum + p[:, b * LANES:(b + 1) * LANES]
                l_ref[idx] = alpha * l_ref[idx] + psum
                acc_ref[idx] = _lane_tile(alpha, DA_V_DIM // LANES) * acc_ref[idx] + _dot(p.astype(BF16), v)
                m_ref[idx] = m_new

    @pl.when(kj < qi)
    def _():
        step(False)

    @pl.when(kj == qi)
    def _():
        step(True)
        lam = (jnp.exp(jnp.sum(lq1_ref[...] * lk1_ref[...], axis=-1, keepdims=True))
               - jnp.exp(jnp.sum(lq2_ref[...] * lk2_ref[...], axis=-1, keepdims=True)) + lambda_init)
        for gq in range(DA_GROUP):
            a0 = acc_ref[gq * 2] / jnp.sum(l_ref[gq * 2], axis=-1, keepdims=True)
            a1 = acc_ref[gq * 2 + 1] / jnp.sum(l_ref[gq * 2 + 1], axis=-1, keepdims=True)
            o = a0 - lam * a1
            ms = jnp.mean(o * o, axis=-1, keepdims=True)
            o = (o * lax.rsqrt(ms + EPS) * sg_ref[...]) * (1.0 - lambda_init)
            o_ref[:, gq * DA_V_DIM:(gq + 1) * DA_V_DIM] = o.astype(o_ref.dtype)


def _attn_call(q, kv, lq1, lk1, lq2, lk2, subln_g, lambda_init):
    S = q.shape[0]
    v_off = K_DIM // DA_V_DIM
    T = ATT_T
    nq = S // T
    pairs = [(i, j) for i in range(nq) for j in range(i + 1)]
    qi_tab = jnp.asarray(np.array([p[0] for p in pairs], np.int32))
    kj_tab = jnp.asarray(np.array([p[1] for p in pairs], np.int32))
    qw = DA_GROUP * 2 * DA_HEAD_DIM
    kw = 2 * DA_HEAD_DIM
    vec = pl.BlockSpec((1, DA_HEAD_DIM), lambda h, s, qt, kt: (0, 0))
    grid_spec = pltpu.PrefetchScalarGridSpec(
        num_scalar_prefetch=2,
        grid=(DA_KV_HEADS, len(pairs)),
        in_specs=[pl.BlockSpec((T, qw), lambda h, s, qt, kt: (qt[s], h)),
                  pl.BlockSpec((T, kw), lambda h, s, qt, kt: (kt[s], h)),
                  pl.BlockSpec((T, DA_V_DIM), lambda h, s, qt, kt: (kt[s], v_off + h)),
                  vec, vec, vec, vec,
                  pl.BlockSpec((1, DA_V_DIM), lambda h, s, qt, kt: (0, 0))],
        out_specs=pl.BlockSpec((T, DA_GROUP * DA_V_DIM), lambda h, s, qt, kt: (qt[s], h)),
        scratch_shapes=[pltpu.VMEM((2 * DA_GROUP, T, LANES), F32),
                        pltpu.VMEM((2 * DA_GROUP, T, LANES), F32),
                        pltpu.VMEM((2 * DA_GROUP, T, DA_V_DIM), F32)],
    )
    return pl.pallas_call(
        functools.partial(_attn_kernel, lambda_init=lambda_init),
        grid_spec=grid_spec,
        out_shape=jax.ShapeDtypeStruct((S, DA_HEADS * DA_V_DIM), BF16),
        compiler_params=_cparams("parallel", "arbitrary"),
        name="diff_attention",
    )(qi_tab, kj_tab, q, kv, kv, lq1, lk1, lq2, lk2, subln_g)


def _router_kernel(x_ref, g_ref, sh_ref, sc_ref, rw_ref, rb_ref, h_ref, idx_ref, wt_ref, cnt_ref, carry_ref,
                   *, n_real):
    i = pl.program_id(0)

    @pl.when(i == 0)
    def _():
        carry_ref[...] = jnp.zeros_like(carry_ref)

    @pl.when(i >= n_real)
    def _():
        h_ref[...] = jnp.zeros_like(h_ref)

    @pl.when(i < n_real)
    def _():
        _route(x_ref, g_ref, sh_ref, sc_ref, rw_ref, rb_ref, h_ref, idx_ref, wt_ref, cnt_ref, carry_ref)


def _route(x_ref, g_ref, sh_ref, sc_ref, rw_ref, rb_ref, h_ref, idx_ref, wt_ref, cnt_ref, carry_ref):
    h = _norm_mod(x_ref[...], g_ref[...], sh_ref[...], sc_ref[...])
    h_ref[...] = h.astype(BF16)
    hh = h.astype(BF16)
    hl = (h - hh.astype(F32)).astype(BF16)
    rw = rw_ref[...]
    wh = rw.astype(BF16)
    wl = (rw - wh.astype(F32)).astype(BF16)
    logits = _dot(hh, wh) + _dot(hl, wh) + _dot(hh, wl)
    lt = logits.T[0:N_EXPERTS, :]
    score = _sigmoid(lt)
    sel = score + rb_ref[...]
    srow = [score[e:e + 1, :] for e in range(N_EXPERTS)]
    row = [sel[e:e + 1, :] for e in range(N_EXPERTS)]

    gscore = []
    for gi in range(N_EXPERT_GROUPS):
        r = row[gi * EXPERTS_PER_GROUP:(gi + 1) * EXPERTS_PER_GROUP]
        best = None
        for a in range(EXPERTS_PER_GROUP):
            for b in range(a + 1, EXPERTS_PER_GROUP):
                s = r[a] + r[b]
                best = s if best is None else jnp.maximum(best, s)
        gscore.append(best)
    bestg = jnp.zeros_like(gscore[0], dtype=jnp.int32)
    bestv = gscore[0]
    for gi in range(1, N_EXPERT_GROUPS):
        upd = gscore[gi] > bestv
        bestg = jnp.where(upd, gi, bestg)
        bestv = jnp.where(upd, gscore[gi], bestv)

    cand = [jnp.where(bestg == e // EXPERTS_PER_GROUP, row[e], -jnp.inf) for e in range(N_EXPERTS)]

    def first_argmax(vals):
        bi = jnp.zeros_like(bestg)
        bv = vals[0]
        for e in range(1, N_EXPERTS):
            upd = vals[e] > bv
            bi = jnp.where(upd, e, bi)
            bv = jnp.where(upd, vals[e], bv)
        return bi

    i1 = first_argmax(cand)
    i2 = first_argmax([jnp.where(i1 == e, -jnp.inf, cand[e]) for e in range(N_EXPERTS)])
    s1 = sum(jnp.where(i1 == e, srow[e], 0.0) for e in range(N_EXPERTS))
    s2 = sum(jnp.where(i2 == e, srow[e], 0.0) for e in range(N_EXPERTS))
    den = s1 + s2
    tm = i1.shape[1]
    e_id = lax.broadcasted_iota(jnp.int32, (LANES, tm), 0)
    oh1 = jnp.where(e_id == i1, 1.0, 0.0)
    oh2 = jnp.where(e_id == i2, 1.0, 0.0)
    oh = (oh1 + oh2).T
    r_t = lax.broadcasted_iota(jnp.int32, (tm, tm), 0)
    c_t = lax.broadcasted_iota(jnp.int32, (tm, tm), 1)
    tril = jnp.where(r_t >= c_t, 1.0, 0.0).astype(BF16)
    incl = _dot(tril, oh.astype(BF16))
    carry = carry_ref[0:1, :]
    base = (incl - oh + carry).T
    rank1 = jnp.sum(oh1 * base, axis=0, keepdims=True).astype(jnp.int32)
    rank2 = jnp.sum(oh2 * base, axis=0, keepdims=True).astype(jnp.int32)
    total = carry + incl[tm - 1:tm, :]
    carry_ref[...] = jnp.broadcast_to(total, carry_ref.shape)
    cnt_ref[...] = jnp.broadcast_to(total, cnt_ref.shape).astype(jnp.int32)
    idx_ref[...] = jnp.concatenate([i1, i2, rank1, rank2, jnp.zeros((4, tm), jnp.int32)], axis=0)
    wt_ref[...] = jnp.concatenate([s1 / den, s2 / den, jnp.zeros((LANES - 2, tm), F32)], axis=0).T


def _router_call(x, g, sh, sc, rw_pad, rb_col, h_rows):
    S, D = x.shape
    tm = ROUTER_TM
    n_real = S // tm
    last = n_real - 1
    vec = pl.BlockSpec((1, D), lambda i: (0, 0))
    return pl.pallas_call(
        functools.partial(_router_kernel, n_real=n_real),
        grid=(h_rows // tm,),
        in_specs=[pl.BlockSpec((tm, D), lambda i: (jnp.minimum(i, last), 0)), vec, vec, vec,
                  pl.BlockSpec((D, LANES), lambda i: (0, 0)),
                  pl.BlockSpec((N_EXPERTS, 1), lambda i: (0, 0))],
        out_specs=[pl.BlockSpec((tm, D), lambda i: (i, 0)),
                   pl.BlockSpec((8, tm), lambda i: (0, jnp.minimum(i, last))),
                   pl.BlockSpec((tm, LANES), lambda i: (jnp.minimum(i, last), 0)),
                   pl.BlockSpec((8, LANES), lambda i: (0, 0))],
        out_shape=[jax.ShapeDtypeStruct((h_rows, D), BF16),
                   jax.ShapeDtypeStruct((8, S), jnp.int32),
                   jax.ShapeDtypeStruct((S, LANES), F32),
                   jax.ShapeDtypeStruct((8, LANES), jnp.int32)],
        scratch_shapes=[pltpu.VMEM((8, LANES), F32)],
        compiler_params=_cparams("arbitrary"),
        name="moe_norm_router",
    )(x, g, sh, sc, rw_pad, rb_col)


def _moe_kernel(ti_ref, te_ref, tf_ref, nv_ref, x_ref, wg_ref, wu_ref, wd_ref, o_ref,
                wgb_ref, wub_ref, wdb_ref):
    t = pl.program_id(0)

    @pl.when(t < nv_ref[0])
    def _():
        @pl.when(tf_ref[t] == 1)
        def _():
            wgb_ref[...] = wg_ref[...].astype(BF16)
            wub_ref[...] = wu_ref[...].astype(BF16)
            wdb_ref[...] = wd_ref[...].astype(BF16)

        x = x_ref[...]
        gt = _dot(x, wgb_ref[...])
        up = _dot(x, wub_ref[...])
        h = (gt * _sigmoid(gt)) * up
        o_ref[...] = _dot(h.astype(BF16), wdb_ref[...]).astype(o_ref.dtype)


def _moe_call(xs, tile_idx, tile_expert, tile_first, n_valid, w_gate, w_up, w_down, layer):
    R, D = xs.shape
    tm = MOE_TM
    F = D_EXPERT
    grid_spec = pltpu.PrefetchScalarGridSpec(
        num_scalar_prefetch=4,
        grid=(R // tm,),
        in_specs=[pl.BlockSpec((tm, D), lambda t, ti, te, tf, nv: (ti[t], 0)),
                  pl.BlockSpec((None, None, D, F), lambda t, ti, te, tf, nv: (layer, te[t], 0, 0)),
                  pl.BlockSpec((None, None, D, F), lambda t, ti, te, tf, nv: (layer, te[t], 0, 0)),
                  pl.BlockSpec((None, None, F, D), lambda t, ti, te, tf, nv: (layer, te[t], 0, 0))],
        out_specs=pl.BlockSpec((tm, D), lambda t, ti, te, tf, nv: (ti[t], 0)),
        scratch_shapes=[pltpu.VMEM((D, F), BF16), pltpu.VMEM((D, F), BF16), pltpu.VMEM((F, D), BF16)],
    )
    return pl.pallas_call(
        _moe_kernel,
        grid_spec=grid_spec,
        out_shape=jax.ShapeDtypeStruct((R, D), BF16),
        compiler_params=_cparams("arbitrary"),
        name="moe_experts",
    )(tile_idx, tile_expert, tile_first, n_valid, xs, w_gate, w_up, w_down)


def _combine_kernel(x_ref, y1_ref, y2_ref, wt_ref, g_ref, fg_ref, o_ref, *, final):
    wt = wt_ref[...]
    y = wt[:, 0:1] * y1_ref[...].astype(F32) + wt[:, 1:2] * y2_ref[...].astype(F32)
    x = x_ref[...] + g_ref[...] * y
    if final:
        ms = jnp.mean(x * x, axis=-1, keepdims=True)
        x = x * lax.rsqrt(ms + EPS) * fg_ref[...]
    o_ref[...] = x


def _combine_call(x, y1, y2, wts, gate, final_g, final):
    S, D = x.shape
    tm = COMBINE_TM
    blk = pl.BlockSpec((tm, D), lambda i: (i, 0))
    vec = pl.BlockSpec((1, D), lambda i: (0, 0))
    return pl.pallas_call(
        functools.partial(_combine_kernel, final=final),
        grid=(S // tm,),
        in_specs=[blk, blk, blk, pl.BlockSpec((tm, LANES), lambda i: (i, 0)), vec, vec],
        out_specs=blk,
        out_shape=jax.ShapeDtypeStruct((S, D), F32),
        compiler_params=_cparams("parallel"),
        name="moe_combine",
    )(x, y1, y2, wts, gate, final_g)


def _take_rows(a, rows):
    return a.at[rows].get(mode="promise_in_bounds")


def _moe_rows(S):
    return TOP_K * S + N_EXPERTS * MOE_TM


def _dispatch_plan(idx, counts, S):
    tm = MOE_TM
    R = _moe_rows(S)
    n_tiles = R // tm
    e_flat = idx[:TOP_K].reshape(-1)
    rank = idx[TOP_K:2 * TOP_K].reshape(-1)
    tok_flat = jnp.tile(jnp.arange(S, dtype=jnp.int32), TOP_K)
    onehot = (e_flat[:, None] == jnp.arange(N_EXPERTS, dtype=jnp.int32)[None, :]).astype(jnp.int32)
    tiles_e = (counts + tm - 1) // tm
    tile_end = jnp.cumsum(tiles_e)
    tile_start = tile_end - tiles_e
    dest = jnp.sum(onehot * (tile_start * tm)[None, :], axis=1) + rank
    src_tok = (jnp.arange(R, dtype=jnp.int32) % S).at[dest].set(
        tok_flat, mode="promise_in_bounds", unique_indices=True)
    n_valid = tile_end[-1]
    t_ids = jnp.minimum(jnp.arange(n_tiles, dtype=jnp.int32), n_valid - 1)
    tile_expert = jnp.sum((t_ids[:, None] >= tile_end[None, :]).astype(jnp.int32), axis=1)
    tile_first = jnp.sum(((t_ids[:, None] == tile_start[None, :]) & (tiles_e[None, :] > 0)).astype(jnp.int32),
                         axis=1)
    return (src_tok, t_ids, tile_expert, tile_first, n_valid.reshape(1).astype(jnp.int32), dest[:S], dest[S:])


def _moe_block(x, g, sh, sc, gate, rw_pad, rb_col, w_gate, w_up, w_down, layer, final_g, final):
    S = x.shape[0]
    h, idx, wts, cnt = _router_call(x, g, sh, sc, rw_pad, rb_col, _moe_rows(S) + ROUTER_TM)
    src_tok, t_ids, t_exp, t_first, n_valid, pos1, pos2 = _dispatch_plan(idx, cnt[0, :N_EXPERTS], S)
    xs = _take_rows(h, src_tok)
    ys = _moe_call(xs, t_ids, t_exp, t_first, n_valid, w_gate, w_up, w_down, layer)
    y1 = _take_rows(ys, pos1)
    y2 = _take_rows(ys, pos2)
    return _combine_call(x, y1, y2, wts, gate, final_g, final)


def _rope_tables(S):
    inv = 1.0 / (ROPE_THETA ** (jnp.arange(0, ROPE_DIM, 2, dtype=F32) / ROPE_DIM))
    ang = jnp.arange(S, dtype=F32)[:, None] * inv[None, :]
    cos, sin = jnp.cos(ang), jnp.sin(ang)
    zeros = jnp.zeros((S, LANES - ROPE_DIM), F32)
    zh = jnp.zeros((S, ROPE_HALF), F32)
    cos_t = jnp.concatenate([cos, cos, jnp.ones((S, LANES - ROPE_DIM), F32)], axis=1)
    sin_a = jnp.concatenate([-sin, zh, zeros], axis=1)
    sin_b = jnp.concatenate([zh, sin, zeros], axis=1)
    return cos_t, sin_a, sin_b


def kernel(x, c, mod_w, mod_b, norm1_g, norm2_g, ssm_in_w, ssm_conv_w, ssm_conv_b, ssm_dt_bias, ssm_a_log, ssm_d, ssm_norm_g, ssm_out_w, kv_mod_w, kv_mod_b, kv_norm_g, w_kv, attn_q_w, lam_q1, lam_k1, lam_q2, lam_k2, subln_g, attn_o_w, moe_w_gate, moe_w_up, moe_w_down, router_w, router_b, final_g):
    _, S, D = x.shape
    xr = x[0]
    cb = jnp.broadcast_to(c[0][:, None], (D, LANES))
    mods = _mod_call(cb, mod_w, mod_b)
    kvm = _mod_call(cb, kv_mod_w[None], kv_mod_b[None])
    cos_t, sin_a, sin_b = _rope_tables(S)
    rw_pad = jnp.pad(router_w, ((0, 0), (0, LANES - N_EXPERTS)))
    rb_col = router_b.reshape(N_EXPERTS, 1)
    row = lambda v: v.reshape(1, -1)
    pad_l = lambda v: jnp.pad(v.reshape(1, -1), ((0, 0), (0, LANES - v.shape[-1])))

    kv = None
    for layer in range(DEPTH):
        m = mods[layer]
        sh1, sc1, g1, sh2, sc2, g2 = [m[:, i * D:(i + 1) * D] for i in range(N_MOD)]
        if layer == N_A_LAYERS:
            kv_sh, kv_sc = kvm[0][:, :D], kvm[0][:, D:]
            kv = _rope_proj_call(xr, row(kv_norm_g), kv_sh, kv_sc, w_kv.astype(BF16), cos_t, sin_a, sin_b,
                                 rope_cols=K_DIM, scale=1.0, name="kv_proj")
        if layer < N_A_LAYERS:
            i = layer
            w_in = ssm_in_w[i]
            zx, dt = _in_proj_call(xr, row(norm1_g[layer]), sh1, sc1,
                                   w_in.astype(BF16),
                                   jnp.pad(w_in[:, ZX_DIM:], ((0, 0), (0, LANES - SSM_HEADS))).astype(BF16),
                                   pad_l(ssm_dt_bias[i]))
            y = _ssd_call(zx, dt, pad_l(ssm_a_log[i]), ssm_conv_w[i], ssm_conv_b[i],
                          row(jnp.repeat(ssm_d[i], SSM_HEAD_DIM)), row(ssm_norm_g[i]))
            xr = _res_mm_call(y, ssm_out_w[i].astype(BF16), xr, g1, name="ssm_out_proj")
        else:
            j = layer - N_A_LAYERS
            lambda_init = 0.8 - 0.6 * float(np.exp(-0.3 * layer))
            q = _rope_proj_call(xr, row(norm1_g[layer]), sh1, sc1, attn_q_w[j].astype(BF16),
                                cos_t, sin_a, sin_b, rope_cols=D, scale=DA_HEAD_DIM ** -0.5 * LOG2E,
                                name="q_proj")
            o = _attn_call(q, kv, row(lam_q1[j]), row(lam_k1[j]), row(lam_q2[j]), row(lam_k2[j]),
                           row(subln_g[j]), lambda_init)
            xr = _res_mm_call(o, attn_o_w[j].astype(BF16), xr, g1, name="attn_o_proj")
        xr = _moe_block(xr, row(norm2_g[layer]), sh2, sc2, g2, rw_pad, rb_col,
                        moe_w_gate, moe_w_up, moe_w_down, layer, row(final_g), layer == DEPTH - 1)
    return xr[None]
```

```python
import functools

import numpy as np
import jax
import jax.numpy as jnp
from jax import lax
from jax.experimental import pallas as pl
from jax.experimental.pallas import tpu as pltpu

F32 = jnp.float32
BF16 = jnp.bfloat16

D_MODEL = 2048
DEPTH = 4
N_A_LAYERS = DEPTH // 2
EPS = 1e-5
N_MOD = 6

D_INNER = 2 * D_MODEL
SSM_HEAD_DIM = 64
SSM_HEADS = D_INNER // SSM_HEAD_DIM
SSM_GROUPS = 8
SSM_HPG = SSM_HEADS // SSM_GROUPS
SSM_STATE = 128
CONV_WIDTH = 4
CHUNK = 256
GROUP_W = SSM_HPG * SSM_HEAD_DIM
ZX_DIM = 2 * D_INNER + 2 * SSM_GROUPS * SSM_STATE

DA_HEADS = 8
DA_KV_HEADS = 4
DA_GROUP = DA_HEADS // DA_KV_HEADS
DA_HEAD_DIM = D_MODEL // DA_HEADS // 2
DA_V_DIM = 2 * DA_HEAD_DIM
K_DIM = DA_KV_HEADS * 2 * DA_HEAD_DIM
V_DIM = DA_KV_HEADS * DA_V_DIM
ROPE_DIM = DA_HEAD_DIM // 4
ROPE_HALF = ROPE_DIM // 2
ROPE_THETA = 500000.0

N_EXPERTS = 16
N_EXPERT_GROUPS = 4
EXPERTS_PER_GROUP = N_EXPERTS // N_EXPERT_GROUPS
TOP_K = 2
D_EXPERT = 512

LOG2E = 1.4426950408889634
LANES = 128
HALO = 16
VMEM_LIMIT = 48 * 1024 * 1024

PROJ_TM = 1024
PROJ_TN = 512
ATT_T = 512
MOE_TM = 256
ROUTER_TM = 512
COMBINE_TM = 512


def _cparams(*sem):
    return pltpu.CompilerParams(dimension_semantics=sem, vmem_limit_bytes=VMEM_LIMIT)


def _sigmoid(v):
    return 1.0 / (1.0 + jnp.exp(-v))


def _silu(v):
    hv = 0.5 * v
    return hv * (1.0 + jnp.tanh(hv))


def _lane_tile(v, n):
    return v if n == 1 else jnp.concatenate([v] * n, axis=1)


def _norm_mod(x, g, sh, sc):
    ms = jnp.mean(x * x, axis=-1, keepdims=True)
    return (x * lax.rsqrt(ms + EPS) * g) * (1.0 + sc) + sh


def _split3(v):
    hi = v.astype(BF16)
    r1 = v - hi.astype(F32)
    mid = r1.astype(BF16)
    lo = (r1 - mid.astype(F32)).astype(BF16)
    return hi, mid, lo


def _dot(a, b):
    return jnp.dot(a, b, preferred_element_type=F32)


def _dot_parts(parts, sel):
    acc = _dot(parts[0], sel)
    for p in parts[1:]:
        acc = acc + _dot(p, sel)
    return acc


def _dot3_l(sel, v):
    hi, mid, lo = _split3(v)
    return _dot(sel, hi) + _dot(sel, mid) + _dot(sel, lo)


def _mod_kernel(c_ref, w_ref, b_ref, o_ref):
    cc = c_ref[...]
    sc = _silu(cc)
    w = w_ref[...]
    o_ref[...] = jnp.sum(w * _lane_tile(sc, w.shape[1] // LANES), axis=0, keepdims=True) + b_ref[...]


def _mod_call(cb, w, b):
    L, D, N = w.shape
    tn = 1024
    return pl.pallas_call(
        _mod_kernel,
        grid=(L, N // tn),
        in_specs=[pl.BlockSpec((D, LANES), lambda l, j: (0, 0)),
                  pl.BlockSpec((None, D, tn), lambda l, j: (l, 0, j)),
                  pl.BlockSpec((None, 1, tn), lambda l, j: (l, 0, j))],
        out_specs=pl.BlockSpec((None, 1, tn), lambda l, j: (l, 0, j)),
        out_shape=jax.ShapeDtypeStruct((L, 1, N), F32),
        compiler_params=_cparams("parallel", "parallel"),
        name="mod_vectors",
    )(cb, w, b.reshape(L, 1, N))


def _norm_prologue(x_ref, g_ref, sh_ref, sc_ref, h_ref):
    tm = x_ref.shape[0]
    rc = 256
    for r in range(0, tm, rc):
        h = _norm_mod(x_ref[r:r + rc, :], g_ref[...], sh_ref[...], sc_ref[...])
        h_ref[r:r + rc, :] = h.astype(BF16)


def _in_proj_kernel(x_ref, g_ref, sh_ref, sc_ref, w_ref, wdt_ref, dtb_ref, o_ref, dt_ref, h_ref):
    @pl.when(pl.program_id(1) == 0)
    def _():
        _norm_prologue(x_ref, g_ref, sh_ref, sc_ref, h_ref)
        d = _dot(h_ref[...], wdt_ref[...]) + dtb_ref[...]
        dt_ref[...] = jnp.maximum(d, 0.0) + jnp.log1p(jnp.exp(-jnp.abs(d)))

    o_ref[...] = _dot(h_ref[...], w_ref[...]).astype(o_ref.dtype)


def _in_proj_call(x, g, sh, sc, w, wdt, dtb):
    M, K = x.shape
    N = ZX_DIM
    tm, tn = PROJ_TM, PROJ_TN
    vec = pl.BlockSpec((1, K), lambda i, j: (0, 0))
    return pl.pallas_call(
        _in_proj_kernel,
        grid=(M // tm, N // tn),
        in_specs=[pl.BlockSpec((tm, K), lambda i, j: (i, 0)), vec, vec, vec,
                  pl.BlockSpec((K, tn), lambda i, j: (0, j)),
                  pl.BlockSpec((K, LANES), lambda i, j: (0, 0)),
                  pl.BlockSpec((1, LANES), lambda i, j: (0, 0))],
        out_specs=[pl.BlockSpec((tm, tn), lambda i, j: (i, j)),
                   pl.BlockSpec((tm, LANES), lambda i, j: (i, 0))],
        out_shape=[jax.ShapeDtypeStruct((M, N), BF16), jax.ShapeDtypeStruct((M, LANES), F32)],
        scratch_shapes=[pltpu.VMEM((tm, K), BF16)],
        compiler_params=_cparams("parallel", "arbitrary"),
        name="ssm_in_proj",
    )(x, g, sh, sc, w, wdt, dtb)


def _rope_proj_kernel(x_ref, g_ref, sh_ref, sc_ref, w_ref, cos_ref, sa_ref, sb_ref, o_ref, h_ref,
                      *, rope_tiles, n_tiles, scale):
    j = pl.program_id(1)

    @pl.when(j == 0)
    def _():
        _norm_prologue(x_ref, g_ref, sh_ref, sc_ref, h_ref)

    acc = _dot(h_ref[...], w_ref[...])
    tn = acc.shape[1]
    nb = tn // LANES

    def roped():
        cos = _lane_tile(cos_ref[...], nb)
        sa = _lane_tile(sa_ref[...], nb)
        sb = _lane_tile(sb_ref[...], nb)
        r = acc * cos + pltpu.roll(acc, tn - ROPE_HALF, 1) * sa + pltpu.roll(acc, ROPE_HALF, 1) * sb
        return (r * scale).astype(o_ref.dtype)

    if rope_tiles >= n_tiles:
        o_ref[...] = roped()
    else:
        @pl.when(j < rope_tiles)
        def _():
            o_ref[...] = roped()

        @pl.when(j >= rope_tiles)
        def _():
            o_ref[...] = acc.astype(o_ref.dtype)


def _rope_proj_call(x, g, sh, sc, w, cos, sa, sb, *, rope_cols, scale, name):
    M, K = x.shape
    N = w.shape[1]
    tm, tn = PROJ_TM, PROJ_TN
    vec = pl.BlockSpec((1, K), lambda i, j: (0, 0))
    tab = pl.BlockSpec((tm, LANES), lambda i, j: (i, 0))
    kern = functools.partial(_rope_proj_kernel, rope_tiles=rope_cols // tn, n_tiles=N // tn, scale=scale)
    return pl.pallas_call(
        kern,
        grid=(M // tm, N // tn),
        in_specs=[pl.BlockSpec((tm, K), lambda i, j: (i, 0)), vec, vec, vec,
                  pl.BlockSpec((K, tn), lambda i, j: (0, j)), tab, tab, tab],
        out_specs=pl.BlockSpec((tm, tn), lambda i, j: (i, j)),
        out_shape=jax.ShapeDtypeStruct((M, N), BF16),
        scratch_shapes=[pltpu.VMEM((tm, K), BF16)],
        compiler_params=_cparams("parallel", "arbitrary"),
        name=name,
    )(x, g, sh, sc, w, cos, sa, sb)


def _res_mm_kernel(a_ref, w_ref, res_ref, gate_ref, o_ref):
    o_ref[...] = res_ref[...] + gate_ref[...] * _dot(a_ref[...], w_ref[...])


def _res_mm_call(a, w, res, gate, *, name):
    M, K = a.shape
    N = w.shape[1]
    tm, tn = PROJ_TM, PROJ_TN
    return pl.pallas_call(
        _res_mm_kernel,
        grid=(M // tm, N // tn),
        in_specs=[pl.BlockSpec((tm, K), lambda i, j: (i, 0)),
                  pl.BlockSpec((K, tn), lambda i, j: (0, j)),
                  pl.BlockSpec((tm, tn), lambda i, j: (i, j)),
                  pl.BlockSpec((1, tn), lambda i, j: (0, j))],
        out_specs=pl.BlockSpec((tm, tn), lambda i, j: (i, j)),
        out_shape=jax.ShapeDtypeStruct((M, N), F32),
        compiler_params=_cparams("parallel", "parallel"),
        name=name,
    )(a, w, res, gate)


def _ssd_kernel(z_ref, x_ref, xh_ref, b_ref, bh_ref, c_ref, ch_ref, dt_ref, alog_ref,
                cwx_ref, cbx_ref, cwb_ref, cbb_ref, cwc_ref, cbc_ref, dsk_ref, ng_ref,
                o_ref, h_ref, cst_ref, shift_ref, sel_ref, esel_ref):
    g = pl.program_id(0)
    c = pl.program_id(1)
    L = CHUNK
    row = lax.broadcasted_iota(jnp.int32, (L, L), 0)
    col = lax.broadcasted_iota(jnp.int32, (L, L), 1)
    causal = row >= col

    @pl.when(c == 0)
    def _():
        h_ref[...] = jnp.zeros_like(h_ref)
        shift_ref[0] = jnp.where(causal, 1.0, 0.0).astype(BF16)
        for k in range(1, CONV_WIDTH):
            shift_ref[k] = jnp.where(row - col == k, 1.0, 0.0).astype(BF16)
        k1 = lax.broadcasted_iota(jnp.int32, (LANES, LANES), 0)
        n1 = lax.broadcasted_iota(jnp.int32, (LANES, LANES), 1)
        sel_ref[...] = jnp.where(jnp.where(n1 < SSM_HPG, k1 - n1, -1) == g * SSM_HPG, 1.0, 0.0).astype(BF16)
        k2 = lax.broadcasted_iota(jnp.int32, (LANES, GROUP_W), 0)
        n2 = lax.broadcasted_iota(jnp.int32, (LANES, GROUP_W), 1) // SSM_HEAD_DIM
        esel_ref[...] = jnp.where(k2 - n2 == g * SSM_HPG, 1.0, 0.0).astype(BF16)

    def conv_silu(cur_ref, halo_ref, w_ref, bias_ref):
        cur_b = cur_ref[...]
        acc = bias_ref[...] + w_ref[CONV_WIDTH - 1:CONV_WIDTH, :] * cur_b.astype(F32)
        prev8 = jnp.where(c == 0, 0.0, halo_ref[...].astype(F32))[HALO - 8:HALO, :]
        row8 = lax.broadcasted_iota(jnp.int32, prev8.shape, 0)
        top = jnp.zeros_like(prev8)
        for k in range(1, CONV_WIDTH):
            wk = w_ref[CONV_WIDTH - 1 - k:CONV_WIDTH - k, :]
            acc = acc + wk * _dot(shift_ref[k], cur_b)
            top = top + wk * jnp.where(row8 < k, pltpu.roll(prev8, k, 0), 0.0)
        acc = jnp.concatenate([acc[0:8, :] + top, acc[8:, :]], axis=0)
        return _silu(acc)

    xs = conv_silu(x_ref, xh_ref, cwx_ref, cbx_ref)
    bm = conv_silu(b_ref, bh_ref, cwb_ref, cbb_ref)
    cm = conv_silu(c_ref, ch_ref, cwc_ref, cbc_ref)

    dt_all = dt_ref[...]
    dta = dt_all * (-jnp.exp(alog_ref[...]))
    cs_all = _dot3_l(shift_ref[0], dta)
    cst_ref[...] = cs_all.T
    cs_r = cst_ref[pl.ds(pl.multiple_of(g * SSM_HPG, SSM_HPG), SSM_HPG), :]

    sel = sel_ref[...]
    esel = esel_ref[...]
    cs_parts = _split3(cs_all)
    cs_g = _dot_parts(cs_parts, sel)
    cs_x = _dot_parts(cs_parts, esel)
    dt_x = _dot_parts(_split3(dt_all)[:2], esel)

    xdt = xs * dt_x
    xb = xdt.astype(BF16)
    bb = bm.astype(BF16)
    cb = cm.astype(BF16)
    cbm = lax.dot_general(cb, bb, (((1,), (1,)), ((), ())), preferred_element_type=F32)

    half = GROUP_W // 2
    lane_head = lax.broadcasted_iota(jnp.int32, (L, half), 1) // SSM_HEAD_DIM
    yd = []
    for blk in range(2):
        xblk = xb[:, blk * half:(blk + 1) * half]
        acc = jnp.zeros((L, half), F32)
        for e4 in range(SSM_HPG // 2):
            e = blk * (SSM_HPG // 2) + e4
            diff = cs_g[:, e:e + 1] - cs_r[e:e + 1, :]
            lm = jnp.exp(jnp.where(causal, diff, -jnp.inf))
            m = (cbm * lm).astype(BF16)
            xm = jnp.where(lane_head == e4, xblk, jnp.zeros_like(xblk))
            acc = acc + _dot(m, xm)
        yd.append(acc)
    y = jnp.concatenate(yd, axis=1)

    hprev = h_ref[...]
    y = y + _dot(cb, hprev.astype(BF16)) * jnp.exp(cs_x)
    cs_last = cs_x[L - 1:L, :]
    xdec = (xdt * jnp.exp(cs_last - cs_x)).astype(BF16)
    states = lax.dot_general(bb, xdec, (((0,), (0,)), ((), ())), preferred_element_type=F32)
    h_ref[...] = hprev * jnp.exp(cs_last) + states

    y = y + xs * dsk_ref[...]
    zz = z_ref[...].astype(F32)
    y = y * _silu(zz)
    ms = jnp.mean(y * y, axis=-1, keepdims=True)
    o_ref[...] = (y * lax.rsqrt(ms + EPS) * ng_ref[...]).astype(o_ref.dtype)


def _ssd_call(zx, dt, alog, conv_w, conv_b, dskip, norm_g):
    S = zx.shape[0]
    G, L, W, N = SSM_GROUPS, CHUNK, GROUP_W, SSM_STATE
    nc = S // L
    xo = D_INNER // W
    bo = 2 * D_INNER // N
    co = bo + G
    hb = L // HALO

    def halo(off):
        return lambda g, c: (jnp.maximum(c * hb - 1, 0), off + g)

    cw = conv_w
    cb = conv_b.reshape(1, -1)
    cxo, cbo, cco = 0, D_INNER // N, D_INNER // N + G
    in_specs = [
        pl.BlockSpec((L, W), lambda g, c: (c, g)),
        pl.BlockSpec((L, W), lambda g, c: (c, xo + g)),
        pl.BlockSpec((HALO, W), halo(xo)),
        pl.BlockSpec((L, N), lambda g, c: (c, bo + g)),
        pl.BlockSpec((HALO, N), halo(bo)),
        pl.BlockSpec((L, N), lambda g, c: (c, co + g)),
        pl.BlockSpec((HALO, N), halo(co)),
        pl.BlockSpec((L, LANES), lambda g, c: (c, 0)),
        pl.BlockSpec((1, LANES), lambda g, c: (0, 0)),
        pl.BlockSpec((CONV_WIDTH, W), lambda g, c: (0, cxo + g)),
        pl.BlockSpec((1, W), lambda g, c: (0, cxo + g)),
        pl.BlockSpec((CONV_WIDTH, N), lambda g, c: (0, cbo + g)),
        pl.BlockSpec((1, N), lambda g, c: (0, cbo + g)),
        pl.BlockSpec((CONV_WIDTH, N), lambda g, c: (0, cco + g)),
        pl.BlockSpec((1, N), lambda g, c: (0, cco + g)),
        pl.BlockSpec((1, W), lambda g, c: (0, g)),
        pl.BlockSpec((1, W), lambda g, c: (0, g)),
    ]
    return pl.pallas_call(
        _ssd_kernel,
        grid=(G, nc),
        in_specs=in_specs,
        out_specs=pl.BlockSpec((L, W), lambda g, c: (c, g)),
        out_shape=jax.ShapeDtypeStruct((S, D_INNER), BF16),
        scratch_shapes=[pltpu.VMEM((N, W), F32), pltpu.VMEM((LANES, L), F32),
                        pltpu.VMEM((CONV_WIDTH, L, L), BF16), pltpu.VMEM((LANES, LANES), BF16),
                        pltpu.VMEM((LANES, W), BF16)],
        compiler_params=_cparams("parallel", "arbitrary"),
        name="ssd_chunk_scan",
    )(zx, zx, zx, zx, zx, zx, zx, dt, alog, cw, cb, cw, cb, cw, cb, dskip, norm_g)


def _attn_kernel(q_ref, k_ref, v_ref, lq1_ref, lk1_ref, lq2_ref, lk2_ref, sg_ref,
                 o_ref, m_ref, l_ref, acc_ref, p_ref, alpha_ref, *, lambda_init):
    qi = pl.program_id(1)
    T = ATT_T
    HD = DA_HEAD_DIM
    m_ref[...] = jnp.full_like(m_ref, -jnp.inf)
    l_ref[...] = jnp.zeros_like(l_ref)
    acc_ref[...] = jnp.zeros_like(acc_ref)

    def softmax_slab(kj, idx, masked):
        r0 = pl.multiple_of(kj * T, T)
        t = idx % 2
        q = q_ref[:, idx * HD:(idx + 1) * HD]
        kt = k_ref[pl.ds(r0, T), t * HD:(t + 1) * HD]
        s = lax.dot_general(q, kt, (((1,), (1,)), ((), ())), preferred_element_type=F32)
        if masked:
            row = lax.broadcasted_iota(jnp.int32, (T, T), 0)
            col = lax.broadcasted_iota(jnp.int32, (T, T), 1)
            s = jnp.where(col <= row, s, -jnp.inf)
        m_prev = m_ref[idx]
        m_new = jnp.maximum(m_prev, jnp.max(s, axis=-1, keepdims=True))
        alpha_ref[idx] = jnp.exp2(m_prev - m_new)
        p = jnp.exp2(s - _lane_tile(m_new, T // LANES))
        psum = p[:, 0:LANES]
        for b in range(1, T // LANES):
            psum = psum + p[:, b * LANES:(b + 1) * LANES]
        l_ref[idx] = alpha_ref[idx] * l_ref[idx] + psum
        p_ref[idx] = p.astype(BF16)
        m_ref[idx] = m_new

    def pv_slab(kj, idx):
        r0 = pl.multiple_of(kj * T, T)
        v = v_ref[pl.ds(r0, T), :]
        acc_ref[idx] = _lane_tile(alpha_ref[idx], DA_V_DIM // LANES) * acc_ref[idx] + _dot(p_ref[idx], v)

    n_slab = 2 * DA_GROUP
    for idx in range(n_slab):
        softmax_slab(qi, idx, True)

    def body(kj, prev):
        for idx in range(n_slab):
            pv_slab(prev, idx)
            softmax_slab(kj, idx, False)
        return kj

    last = lax.fori_loop(0, qi, body, qi)
    for idx in range(n_slab):
        pv_slab(last, idx)

    lam = (jnp.exp(jnp.sum(lq1_ref[...] * lk1_ref[...], axis=-1, keepdims=True))
           - jnp.exp(jnp.sum(lq2_ref[...] * lk2_ref[...], axis=-1, keepdims=True)) + lambda_init)
    for gq in range(DA_GROUP):
        a0 = acc_ref[gq * 2] / jnp.sum(l_ref[gq * 2], axis=-1, keepdims=True)
        a1 = acc_ref[gq * 2 + 1] / jnp.sum(l_ref[gq * 2 + 1], axis=-1, keepdims=True)
        o = a0 - lam * a1
        ms = jnp.mean(o * o, axis=-1, keepdims=True)
        o = (o * lax.rsqrt(ms + EPS) * sg_ref[...]) * (1.0 - lambda_init)
        o_ref[:, gq * DA_V_DIM:(gq + 1) * DA_V_DIM] = o.astype(o_ref.dtype)


def _attn_call(q, kv, lq1, lk1, lq2, lk2, subln_g, lambda_init):
    S = q.shape[0]
    v_off = K_DIM // DA_V_DIM
    T = ATT_T
    qw = DA_GROUP * 2 * DA_HEAD_DIM
    kw = 2 * DA_HEAD_DIM
    vec = pl.BlockSpec((1, DA_HEAD_DIM), lambda h, i: (0, 0))
    return pl.pallas_call(
        functools.partial(_attn_kernel, lambda_init=lambda_init),
        grid=(DA_KV_HEADS, S // T),
        in_specs=[pl.BlockSpec((T, qw), lambda h, i: (i, h)),
                  pl.BlockSpec((S, kw), lambda h, i: (0, h)),
                  pl.BlockSpec((S, DA_V_DIM), lambda h, i: (0, v_off + h)),
                  vec, vec, vec, vec,
                  pl.BlockSpec((1, DA_V_DIM), lambda h, i: (0, 0))],
        out_specs=pl.BlockSpec((T, DA_GROUP * DA_V_DIM), lambda h, i: (i, h)),
        out_shape=jax.ShapeDtypeStruct((S, DA_HEADS * DA_V_DIM), BF16),
        scratch_shapes=[pltpu.VMEM((2 * DA_GROUP, T, LANES), F32),
                        pltpu.VMEM((2 * DA_GROUP, T, LANES), F32),
                        pltpu.VMEM((2 * DA_GROUP, T, DA_V_DIM), F32),
                        pltpu.VMEM((2 * DA_GROUP, T, T), BF16),
                        pltpu.VMEM((2 * DA_GROUP, T, LANES), F32)],
        compiler_params=_cparams("parallel", "parallel"),
        name="diff_attention",
    )(q, kv, kv, lq1, lk1, lq2, lk2, subln_g)


def _router_kernel(x_ref, g_ref, sh_ref, sc_ref, rw_ref, rb_ref, h_ref, idx_ref, wt_ref, cnt_ref, carry_ref,
                   *, n_real):
    i = pl.program_id(0)

    @pl.when(i == 0)
    def _():
        carry_ref[...] = jnp.zeros_like(carry_ref)

    @pl.when(i >= n_real)
    def _():
        h_ref[...] = jnp.zeros_like(h_ref)

    @pl.when(i < n_real)
    def _():
        _route(x_ref, g_ref, sh_ref, sc_ref, rw_ref, rb_ref, h_ref, idx_ref, wt_ref, cnt_ref, carry_ref)


def _route(x_ref, g_ref, sh_ref, sc_ref, rw_ref, rb_ref, h_ref, idx_ref, wt_ref, cnt_ref, carry_ref):
    h = _norm_mod(x_ref[...], g_ref[...], sh_ref[...], sc_ref[...])
    h_ref[...] = h.astype(BF16)
    hh = h.astype(BF16)
    hl = (h - hh.astype(F32)).astype(BF16)
    rw = rw_ref[...]
    wh = rw.astype(BF16)
    wl = (rw - wh.astype(F32)).astype(BF16)
    logits = _dot(hh, wh) + _dot(hl, wh) + _dot(hh, wl)
    lt = logits.T[0:N_EXPERTS, :]
    score = _sigmoid(lt)
    sel = score + rb_ref[...]
    srow = [score[e:e + 1, :] for e in range(N_EXPERTS)]
    row = [sel[e:e + 1, :] for e in range(N_EXPERTS)]

    gscore = []
    for gi in range(N_EXPERT_GROUPS):
        r = row[gi * EXPERTS_PER_GROUP:(gi + 1) * EXPERTS_PER_GROUP]
        best = None
        for a in range(EXPERTS_PER_GROUP):
            for b in range(a + 1, EXPERTS_PER_GROUP):
                s = r[a] + r[b]
                best = s if best is None else jnp.maximum(best, s)
        gscore.append(best)
    bestg = jnp.zeros_like(gscore[0], dtype=jnp.int32)
    bestv = gscore[0]
    for gi in range(1, N_EXPERT_GROUPS):
        upd = gscore[gi] > bestv
        bestg = jnp.where(upd, gi, bestg)
        bestv = jnp.where(upd, gscore[gi], bestv)

    cand = [jnp.where(bestg == e // EXPERTS_PER_GROUP, row[e], -jnp.inf) for e in range(N_EXPERTS)]

    def first_argmax(vals):
        bi = jnp.zeros_like(bestg)
        bv = vals[0]
        for e in range(1, N_EXPERTS):
            upd = vals[e] > bv
            bi = jnp.where(upd, e, bi)
            bv = jnp.where(upd, vals[e], bv)
        return bi

    i1 = first_argmax(cand)
    i2 = first_argmax([jnp.where(i1 == e, -jnp.inf, cand[e]) for e in range(N_EXPERTS)])
    s1 = sum(jnp.where(i1 == e, srow[e], 0.0) for e in range(N_EXPERTS))
    s2 = sum(jnp.where(i2 == e, srow[e], 0.0) for e in range(N_EXPERTS))
    den = s1 + s2
    tm = i1.shape[1]
    e_id = lax.broadcasted_iota(jnp.int32, (LANES, tm), 0)
    oh1 = jnp.where(e_id == i1, 1.0, 0.0)
    oh2 = jnp.where(e_id == i2, 1.0, 0.0)
    oh = (oh1 + oh2).T
    r_t = lax.broadcasted_iota(jnp.int32, (tm, tm), 0)
    c_t = lax.broadcasted_iota(jnp.int32, (tm, tm), 1)
    tril = jnp.where(r_t >= c_t, 1.0, 0.0).astype(BF16)
    incl = _dot(tril, oh.astype(BF16))
    carry = carry_ref[0:1, :]
    base = (incl - oh + carry).T
    rank1 = jnp.sum(oh1 * base, axis=0, keepdims=True).astype(jnp.int32)
    rank2 = jnp.sum(oh2 * base, axis=0, keepdims=True).astype(jnp.int32)
    total = carry + incl[tm - 1:tm, :]
    carry_ref[...] = jnp.broadcast_to(total, carry_ref.shape)
    cnt_ref[...] = jnp.broadcast_to(total, cnt_ref.shape).astype(jnp.int32)
    idx_ref[...] = jnp.concatenate([i1, i2, rank1, rank2, jnp.zeros((4, tm), jnp.int32)], axis=0)
    wt_ref[...] = jnp.concatenate([s1 / den, s2 / den, jnp.zeros((LANES - 2, tm), F32)], axis=0).T


def _router_call(x, g, sh, sc, rw_pad, rb_col, h_rows):
    S, D = x.shape
    tm = ROUTER_TM
    n_real = S // tm
    last = n_real - 1
    vec = pl.BlockSpec((1, D), lambda i: (0, 0))
    return pl.pallas_call(
        functools.partial(_router_kernel, n_real=n_real),
        grid=(h_rows // tm,),
        in_specs=[pl.BlockSpec((tm, D), lambda i: (jnp.minimum(i, last), 0)), vec, vec, vec,
                  pl.BlockSpec((D, LANES), lambda i: (0, 0)),
                  pl.BlockSpec((N_EXPERTS, 1), lambda i: (0, 0))],
        out_specs=[pl.BlockSpec((tm, D), lambda i: (i, 0)),
                   pl.BlockSpec((8, tm), lambda i: (0, jnp.minimum(i, last))),
                   pl.BlockSpec((tm, LANES), lambda i: (jnp.minimum(i, last), 0)),
                   pl.BlockSpec((8, LANES), lambda i: (0, 0))],
        out_shape=[jax.ShapeDtypeStruct((h_rows, D), BF16),
                   jax.ShapeDtypeStruct((8, S), jnp.int32),
                   jax.ShapeDtypeStruct((S, LANES), F32),
                   jax.ShapeDtypeStruct((8, LANES), jnp.int32)],
        scratch_shapes=[pltpu.VMEM((8, LANES), F32)],
        compiler_params=_cparams("arbitrary"),
        name="moe_norm_router",
    )(x, g, sh, sc, rw_pad, rb_col)


def _moe_kernel(ti_ref, te_ref, tf_ref, nv_ref, x_ref, wg_ref, wu_ref, wd_ref, o_ref,
                wgb_ref, wub_ref, wdb_ref):
    t = pl.program_id(0)

    @pl.when(t < nv_ref[0])
    def _():
        @pl.when(tf_ref[t] == 1)
        def _():
            wgb_ref[...] = wg_ref[...].astype(BF16)
            wub_ref[...] = wu_ref[...].astype(BF16)
            wdb_ref[...] = wd_ref[...].astype(BF16)

        x = x_ref[...]
        gt = _dot(x, wgb_ref[...])
        up = _dot(x, wub_ref[...])
        h = _silu(gt) * up
        o_ref[...] = _dot(h.astype(BF16), wdb_ref[...]).astype(o_ref.dtype)

    @pl.when(t >= nv_ref[0])
    def _():
        o_ref[...] = jnp.zeros_like(o_ref)


def _moe_call(xs, tile_idx, tile_expert, tile_first, n_valid, w_gate, w_up, w_down, layer):
    R, D = xs.shape
    tm = MOE_TM
    F = D_EXPERT
    grid_spec = pltpu.PrefetchScalarGridSpec(
        num_scalar_prefetch=4,
        grid=(R // tm,),
        in_specs=[pl.BlockSpec((tm, D), lambda t, ti, te, tf, nv: (ti[t], 0)),
                  pl.BlockSpec((None, None, D, F), lambda t, ti, te, tf, nv: (layer, te[t], 0, 0)),
                  pl.BlockSpec((None, None, D, F), lambda t, ti, te, tf, nv: (layer, te[t], 0, 0)),
                  pl.BlockSpec((None, None, F, D), lambda t, ti, te, tf, nv: (layer, te[t], 0, 0))],
        out_specs=pl.BlockSpec((tm, D), lambda t, ti, te, tf, nv: (t, 0)),
        scratch_shapes=[pltpu.VMEM((D, F), BF16), pltpu.VMEM((D, F), BF16), pltpu.VMEM((F, D), BF16)],
    )
    return pl.pallas_call(
        _moe_kernel,
        grid_spec=grid_spec,
        out_shape=jax.ShapeDtypeStruct((R, D), BF16),
        compiler_params=_cparams("arbitrary"),
        name="moe_experts",
    )(tile_idx, tile_expert, tile_first, n_valid, xs, w_gate, w_up, w_down)


def _combine_kernel(x_ref, y1_ref, y2_ref, wt_ref, g_ref, fg_ref, o_ref, *, final):
    wt = wt_ref[...]
    y = wt[:, 0:1] * y1_ref[...].astype(F32) + wt[:, 1:2] * y2_ref[...].astype(F32)
    x = x_ref[...] + g_ref[...] * y
    if final:
        ms = jnp.mean(x * x, axis=-1, keepdims=True)
        x = x * lax.rsqrt(ms + EPS) * fg_ref[...]
    o_ref[...] = x


def _combine_call(x, y1, y2, wts, gate, final_g, final):
    S, D = x.shape
    tm = COMBINE_TM
    blk = pl.BlockSpec((tm, D), lambda i: (i, 0))
    vec = pl.BlockSpec((1, D), lambda i: (0, 0))
    return pl.pallas_call(
        functools.partial(_combine_kernel, final=final),
        grid=(S // tm,),
        in_specs=[blk, blk, blk, pl.BlockSpec((tm, LANES), lambda i: (i, 0)), vec, vec],
        out_specs=blk,
        out_shape=jax.ShapeDtypeStruct((S, D), F32),
        compiler_params=_cparams("parallel"),
        name="moe_combine",
    )(x, y1, y2, wts, gate, final_g)


def _take_rows(a, rows):
    return a.at[rows].get(mode="promise_in_bounds")


def _moe_rows(S):
    return TOP_K * S + N_EXPERTS * MOE_TM


def _dispatch_plan(idx, counts, S):
    tm = MOE_TM
    R = _moe_rows(S)
    n_tiles = R // tm
    e_flat = idx[:TOP_K].reshape(-1)
    rank = idx[TOP_K:2 * TOP_K].reshape(-1)
    tok_flat = jnp.tile(jnp.arange(S, dtype=jnp.int32), TOP_K)
    onehot = (e_flat[:, None] == jnp.arange(N_EXPERTS, dtype=jnp.int32)[None, :]).astype(jnp.int32)
    tiles_e = (counts + tm - 1) // tm
    tile_end = jnp.cumsum(tiles_e)
    tile_start = tile_end - tiles_e
    dest = jnp.sum(onehot * (tile_start * tm)[None, :], axis=1) + rank
    src_tok = (jnp.arange(R, dtype=jnp.int32) % S).at[dest].set(
        tok_flat, mode="promise_in_bounds", unique_indices=True)
    n_valid = tile_end[-1]
    t_ids = jnp.minimum(jnp.arange(n_tiles, dtype=jnp.int32), n_valid - 1)
    tile_expert = jnp.sum((t_ids[:, None] >= tile_end[None, :]).astype(jnp.int32), axis=1)
    tile_first = jnp.sum(((t_ids[:, None] == tile_start[None, :]) & (tiles_e[None, :] > 0)).astype(jnp.int32),
                         axis=1)
    return (src_tok, t_ids, tile_expert, tile_first, n_valid.reshape(1).astype(jnp.int32), dest[:S], dest[S:])


def _moe_block(x, g, sh, sc, gate, rw_pad, rb_col, w_gate, w_up, w_down, layer, final_g, final):
    S = x.shape[0]
    h, idx, wts, cnt = _router_call(x, g, sh, sc, rw_pad, rb_col, _moe_rows(S) + ROUTER_TM)
    src_tok, t_ids, t_exp, t_first, n_valid, pos1, pos2 = _dispatch_plan(idx, cnt[0, :N_EXPERTS], S)
    xs = _take_rows(h, src_tok)
    ys = _moe_call(xs, t_ids, t_exp, t_first, n_valid, w_gate, w_up, w_down, layer)
    y1 = _take_rows(ys, pos1)
    y2 = _take_rows(ys, pos2)
    return _combine_call(x, y1, y2, wts, gate, final_g, final)


def _rope_tables(S):
    inv = 1.0 / (ROPE_THETA ** (jnp.arange(0, ROPE_DIM, 2, dtype=F32) / ROPE_DIM))
    ang = jnp.arange(S, dtype=F32)[:, None] * inv[None, :]
    cos, sin = jnp.cos(ang), jnp.sin(ang)
    zeros = jnp.zeros((S, LANES - ROPE_DIM), F32)
    zh = jnp.zeros((S, ROPE_HALF), F32)
    cos_t = jnp.concatenate([cos, cos, jnp.ones((S, LANES - ROPE_DIM), F32)], axis=1)
    sin_a = jnp.concatenate([-sin, zh, zeros], axis=1)
    sin_b = jnp.concatenate([zh, sin, zeros], axis=1)
    return cos_t, sin_a, sin_b


def kernel(x, c, mod_w, mod_b, norm1_g, norm2_g, ssm_in_w, ssm_conv_w, ssm_conv_b, ssm_dt_bias, ssm_a_log, ssm_d, ssm_norm_g, ssm_out_w, kv_mod_w, kv_mod_b, kv_norm_g, w_kv, attn_q_w, lam_q1, lam_k1, lam_q2, lam_k2, subln_g, attn_o_w, moe_w_gate, moe_w_up, moe_w_down, router_w, router_b, final_g):
    _, S, D = x.shape
    xr = x[0]
    cb = jnp.broadcast_to(c[0][:, None], (D, LANES))
    mods = _mod_call(cb, mod_w, mod_b)
    kvm = _mod_call(cb, kv_mod_w[None], kv_mod_b[None])
    cos_t, sin_a, sin_b = _rope_tables(S)
    rw_pad = jnp.pad(router_w, ((0, 0), (0, LANES - N_EXPERTS)))
    rb_col = router_b.reshape(N_EXPERTS, 1)
    row = lambda v: v.reshape(1, -1)
    pad_l = lambda v: jnp.pad(v.reshape(1, -1), ((0, 0), (0, LANES - v.shape[-1])))

    kv = None
    for layer in range(DEPTH):
        m = mods[layer]
        sh1, sc1, g1, sh2, sc2, g2 = [m[:, i * D:(i + 1) * D] for i in range(N_MOD)]
        if layer == N_A_LAYERS:
            kv_sh, kv_sc = kvm[0][:, :D], kvm[0][:, D:]
            kv = _rope_proj_call(xr, row(kv_norm_g), kv_sh, kv_sc, w_kv.astype(BF16), cos_t, sin_a, sin_b,
                                 rope_cols=K_DIM, scale=1.0, name="kv_proj")
        if layer < N_A_LAYERS:
            i = layer
            w_in = ssm_in_w[i]
            zx, dt = _in_proj_call(xr, row(norm1_g[layer]), sh1, sc1,
                                   w_in.astype(BF16),
                                   jnp.pad(w_in[:, ZX_DIM:], ((0, 0), (0, LANES - SSM_HEADS))).astype(BF16),
                                   pad_l(ssm_dt_bias[i]))
            y = _ssd_call(zx, dt, pad_l(ssm_a_log[i]), ssm_conv_w[i], ssm_conv_b[i],
                          row(jnp.repeat(ssm_d[i], SSM_HEAD_DIM)), row(ssm_norm_g[i]))
            xr = _res_mm_call(y, ssm_out_w[i].astype(BF16), xr, g1, name="ssm_out_proj")
        else:
            j = layer - N_A_LAYERS
            lambda_init = 0.8 - 0.6 * float(np.exp(-0.3 * layer))
            q = _rope_proj_call(xr, row(norm1_g[layer]), sh1, sc1, attn_q_w[j].astype(BF16),
                                cos_t, sin_a, sin_b, rope_cols=D, scale=DA_HEAD_DIM ** -0.5 * LOG2E,
                                name="q_proj")
            o = _attn_call(q, kv, row(lam_q1[j]), row(lam_k1[j]), row(lam_q2[j]), row(lam_k2[j]),
                           row(subln_g[j]), lambda_init)
            xr = _res_mm_call(o, attn_o_w[j].astype(BF16), xr, g1, name="attn_o_proj")
        xr = _moe_block(xr, row(norm2_g[layer]), sh2, sc2, g2, rw_pad, rb_col,
                        moe_w_gate, moe_w_up, moe_w_down, layer, row(final_g), layer == DEPTH - 1)
    return xr[None]
```

```python
import functools

import numpy as np
import jax
import jax.numpy as jnp
from jax import lax
from jax.experimental import pallas as pl
from jax.experimental.pallas import tpu as pltpu

F32 = jnp.float32
BF16 = jnp.bfloat16

D_MODEL = 2048
DEPTH = 4
N_A_LAYERS = DEPTH // 2
EPS = 1e-5
N_MOD = 6

D_INNER = 2 * D_MODEL
SSM_HEAD_DIM = 64
SSM_HEADS = D_INNER // SSM_HEAD_DIM
SSM_GROUPS = 8
SSM_HPG = SSM_HEADS // SSM_GROUPS
SSM_STATE = 128
CONV_WIDTH = 4
CHUNK = 256
GROUP_W = SSM_HPG * SSM_HEAD_DIM
ZX_DIM = 2 * D_INNER + 2 * SSM_GROUPS * SSM_STATE

DA_HEADS = 8
DA_KV_HEADS = 4
DA_GROUP = DA_HEADS // DA_KV_HEADS
DA_HEAD_DIM = D_MODEL // DA_HEADS // 2
DA_V_DIM = 2 * DA_HEAD_DIM
K_DIM = DA_KV_HEADS * 2 * DA_HEAD_DIM
V_DIM = DA_KV_HEADS * DA_V_DIM
ROPE_DIM = DA_HEAD_DIM // 4
ROPE_HALF = ROPE_DIM // 2
ROPE_THETA = 500000.0

N_EXPERTS = 16
N_EXPERT_GROUPS = 4
EXPERTS_PER_GROUP = N_EXPERTS // N_EXPERT_GROUPS
TOP_K = 2
D_EXPERT = 512

LOG2E = 1.4426950408889634
LANES = 128
HALO = 16
VMEM_LIMIT = 48 * 1024 * 1024

PROJ_TM = 1024
PROJ_TN = 512
ATT_T = 512
MOE_TM = 256
SSD_GPS = 4
ROUTER_TM = 512
COMBINE_TM = 512
COMBINE_PARTS = 2


def _cparams(*sem):
    return pltpu.CompilerParams(dimension_semantics=sem, vmem_limit_bytes=VMEM_LIMIT)


def _sigmoid(v):
    return 1.0 / (1.0 + jnp.exp(-v))


def _silu(v):
    hv = 0.5 * v
    return hv * (1.0 + jnp.tanh(hv))


def _lane_tile(v, n):
    return v if n == 1 else jnp.concatenate([v] * n, axis=1)


def _norm_mod(x, g, sh, sc):
    ms = jnp.mean(x * x, axis=-1, keepdims=True)
    return (x * lax.rsqrt(ms + EPS) * g) * (1.0 + sc) + sh


def _split3(v):
    hi = v.astype(BF16)
    r1 = v - hi.astype(F32)
    mid = r1.astype(BF16)
    lo = (r1 - mid.astype(F32)).astype(BF16)
    return hi, mid, lo


def _dot(a, b):
    return jnp.dot(a, b, preferred_element_type=F32)


def _dot_parts(parts, sel):
    acc = _dot(parts[0], sel)
    for p in parts[1:]:
        acc = acc + _dot(p, sel)
    return acc


def _dot3_l(sel, v):
    hi, mid, lo = _split3(v)
    return _dot(sel, hi) + _dot(sel, mid) + _dot(sel, lo)


def _mod_kernel(c_ref, w_ref, b_ref, o_ref):
    cc = c_ref[...]
    sc = _silu(cc)
    w = w_ref[...]
    o_ref[...] = jnp.sum(w * _lane_tile(sc, w.shape[1] // LANES), axis=0, keepdims=True) + b_ref[...]


def _mod_call(cb, w, b):
    L, D, N = w.shape
    tn = 1024
    return pl.pallas_call(
        _mod_kernel,
        grid=(L, N // tn),
        in_specs=[pl.BlockSpec((D, LANES), lambda l, j: (0, 0)),
                  pl.BlockSpec((None, D, tn), lambda l, j: (l, 0, j)),
                  pl.BlockSpec((None, 1, tn), lambda l, j: (l, 0, j))],
        out_specs=pl.BlockSpec((None, 1, tn), lambda l, j: (l, 0, j)),
        out_shape=jax.ShapeDtypeStruct((L, 1, N), F32),
        compiler_params=_cparams("parallel", "parallel"),
        name="mod_vectors",
    )(cb, w, b.reshape(L, 1, N))


def _norm_prologue(x_ref, g_ref, sh_ref, sc_ref, h_ref):
    tm = x_ref.shape[0]
    rc = 256
    for r in range(0, tm, rc):
        h = _norm_mod(x_ref[r:r + rc, :], g_ref[...], sh_ref[...], sc_ref[...])
        h_ref[r:r + rc, :] = h.astype(BF16)


def _in_proj_kernel(x_ref, g_ref, sh_ref, sc_ref, w_ref, wdt_ref, dtb_ref, o_ref, dt_ref, h_ref):
    @pl.when(pl.program_id(1) == 0)
    def _():
        _norm_prologue(x_ref, g_ref, sh_ref, sc_ref, h_ref)
        d = _dot(h_ref[...], wdt_ref[...]) + dtb_ref[...]
        dt_ref[...] = jnp.maximum(d, 0.0) + jnp.log1p(jnp.exp(-jnp.abs(d)))

    o_ref[...] = _dot(h_ref[...], w_ref[...]).astype(o_ref.dtype)


def _in_proj_call(x, g, sh, sc, w, wdt, dtb):
    M, K = x.shape
    N = ZX_DIM
    tm, tn = PROJ_TM, PROJ_TN
    vec = pl.BlockSpec((1, K), lambda i, j: (0, 0))
    return pl.pallas_call(
        _in_proj_kernel,
        grid=(M // tm, N // tn),
        in_specs=[pl.BlockSpec((tm, K), lambda i, j: (i, 0)), vec, vec, vec,
                  pl.BlockSpec((K, tn), lambda i, j: (0, j)),
                  pl.BlockSpec((K, LANES), lambda i, j: (0, 0)),
                  pl.BlockSpec((1, LANES), lambda i, j: (0, 0))],
        out_specs=[pl.BlockSpec((tm, tn), lambda i, j: (i, j)),
                   pl.BlockSpec((tm, LANES), lambda i, j: (i, 0))],
        out_shape=[jax.ShapeDtypeStruct((M, N), BF16), jax.ShapeDtypeStruct((M, LANES), F32)],
        scratch_shapes=[pltpu.VMEM((tm, K), BF16)],
        compiler_params=_cparams("parallel", "arbitrary"),
        name="ssm_in_proj",
    )(x, g, sh, sc, w, wdt, dtb)


def _rope_proj_kernel(x_ref, g_ref, sh_ref, sc_ref, w_ref, cos_ref, sa_ref, sb_ref, o_ref, h_ref,
                      *, rope_tiles, n_tiles, scale):
    j = pl.program_id(1)

    @pl.when(j == 0)
    def _():
        _norm_prologue(x_ref, g_ref, sh_ref, sc_ref, h_ref)

    acc = _dot(h_ref[...], w_ref[...])
    tn = acc.shape[1]
    nb = tn // LANES

    def roped():
        cos = _lane_tile(cos_ref[...], nb)
        sa = _lane_tile(sa_ref[...], nb)
        sb = _lane_tile(sb_ref[...], nb)
        r = acc * cos + pltpu.roll(acc, tn - ROPE_HALF, 1) * sa + pltpu.roll(acc, ROPE_HALF, 1) * sb
        return (r * scale).astype(o_ref.dtype)

    if rope_tiles >= n_tiles:
        o_ref[...] = roped()
    else:
        @pl.when(j < rope_tiles)
        def _():
            o_ref[...] = roped()

        @pl.when(j >= rope_tiles)
        def _():
            o_ref[...] = acc.astype(o_ref.dtype)


def _rope_proj_call(x, g, sh, sc, w, cos, sa, sb, *, rope_cols, scale, name):
    M, K = x.shape
    N = w.shape[1]
    tm, tn = PROJ_TM, PROJ_TN
    vec = pl.BlockSpec((1, K), lambda i, j: (0, 0))
    tab = pl.BlockSpec((tm, LANES), lambda i, j: (i, 0))
    kern = functools.partial(_rope_proj_kernel, rope_tiles=rope_cols // tn, n_tiles=N // tn, scale=scale)
    return pl.pallas_call(
        kern,
        grid=(M // tm, N // tn),
        in_specs=[pl.BlockSpec((tm, K), lambda i, j: (i, 0)), vec, vec, vec,
                  pl.BlockSpec((K, tn), lambda i, j: (0, j)), tab, tab, tab],
        out_specs=pl.BlockSpec((tm, tn), lambda i, j: (i, j)),
        out_shape=jax.ShapeDtypeStruct((M, N), BF16),
        scratch_shapes=[pltpu.VMEM((tm, K), BF16)],
        compiler_params=_cparams("parallel", "arbitrary"),
        name=name,
    )(x, g, sh, sc, w, cos, sa, sb)


def _res_mm_kernel(a_ref, w_ref, res_ref, gate_ref, o_ref):
    o_ref[...] = res_ref[...] + gate_ref[...] * _dot(a_ref[...], w_ref[...])


def _res_mm_call(a, w, res, gate, *, name):
    M, K = a.shape
    N = w.shape[1]
    tm, tn = PROJ_TM, PROJ_TN
    return pl.pallas_call(
        _res_mm_kernel,
        grid=(M // tm, N // tn),
        in_specs=[pl.BlockSpec((tm, K), lambda i, j: (i, 0)),
                  pl.BlockSpec((K, tn), lambda i, j: (0, j)),
                  pl.BlockSpec((tm, tn), lambda i, j: (i, j)),
                  pl.BlockSpec((1, tn), lambda i, j: (0, j))],
        out_specs=pl.BlockSpec((tm, tn), lambda i, j: (i, j)),
        out_shape=jax.ShapeDtypeStruct((M, N), F32),
        compiler_params=_cparams("parallel", "parallel"),
        name=name,
    )(a, w, res, gate)


def _ssd_kernel(z_ref, x_ref, xh_ref, b_ref, bh_ref, c_ref, ch_ref, dt_ref, alog_ref,
                cwx_ref, cbx_ref, cwb_ref, cbb_ref, cwc_ref, cbc_ref, dsk_ref, ng_ref,
                o_ref, h_ref, cst_ref, shift_ref, sel_ref, esel_ref):
    gp = pl.program_id(0)
    c = pl.program_id(1)
    L, W, N = CHUNK, GROUP_W, SSM_STATE
    row = lax.broadcasted_iota(jnp.int32, (L, L), 0)
    col = lax.broadcasted_iota(jnp.int32, (L, L), 1)
    causal = row >= col

    @pl.when(c == 0)
    def _():
        h_ref[...] = jnp.zeros_like(h_ref)
        shift_ref[0] = jnp.where(causal, 1.0, 0.0).astype(BF16)
        for k in range(1, CONV_WIDTH):
            shift_ref[k] = jnp.where(row - col == k, 1.0, 0.0).astype(BF16)
        k1 = lax.broadcasted_iota(jnp.int32, (LANES, LANES), 0)
        n1 = lax.broadcasted_iota(jnp.int32, (LANES, LANES), 1)
        k2 = lax.broadcasted_iota(jnp.int32, (LANES, W), 0)
        n2 = lax.broadcasted_iota(jnp.int32, (LANES, W), 1) // SSM_HEAD_DIM
        for j in range(SSD_GPS):
            h0 = (gp * SSD_GPS + j) * SSM_HPG
            sel_ref[j] = jnp.where(jnp.where(n1 < SSM_HPG, k1 - n1, -1) == h0, 1.0, 0.0).astype(BF16)
            esel_ref[j] = jnp.where(k2 - n2 == h0, 1.0, 0.0).astype(BF16)

    def conv_silu(cur_b, halo_b, w_ref, bias, lo, hi):
        acc = bias + w_ref[CONV_WIDTH - 1:CONV_WIDTH, lo:hi] * cur_b.astype(F32)
        prev8 = jnp.where(c == 0, 0.0, halo_b.astype(F32))[HALO - 8:HALO, :]
        row8 = lax.broadcasted_iota(jnp.int32, prev8.shape, 0)
        top = jnp.zeros_like(prev8)
        for k in range(1, CONV_WIDTH):
            wk = w_ref[CONV_WIDTH - 1 - k:CONV_WIDTH - k, lo:hi]
            acc = acc + wk * _dot(shift_ref[k], cur_b)
            top = top + wk * jnp.where(row8 < k, pltpu.roll(prev8, k, 0), 0.0)
        acc = jnp.concatenate([acc[0:8, :] + top, acc[8:, :]], axis=0)
        return _silu(acc)

    dt_all = dt_ref[...]
    dta = dt_all * (-jnp.exp(alog_ref[...]))
    cs_all = _dot3_l(shift_ref[0], dta)
    cst_ref[...] = cs_all.T
    cs_parts = _split3(cs_all)
    dt_parts = _split3(dt_all)[:2]
    half = W // 2
    lane_head = lax.broadcasted_iota(jnp.int32, (L, half), 1) // SSM_HEAD_DIM

    for j in range(SSD_GPS):
        wl, wh = j * W, (j + 1) * W
        nl, nh = j * N, (j + 1) * N
        xs = conv_silu(x_ref[:, wl:wh], xh_ref[:, wl:wh], cwx_ref, cbx_ref[:, wl:wh], wl, wh)
        bm = conv_silu(b_ref[:, nl:nh], bh_ref[:, nl:nh], cwb_ref, cbb_ref[:, nl:nh], nl, nh)
        cm = conv_silu(c_ref[:, nl:nh], ch_ref[:, nl:nh], cwc_ref, cbc_ref[:, nl:nh], nl, nh)

        h0 = pl.multiple_of((gp * SSD_GPS + j) * SSM_HPG, SSM_HPG)
        cs_r = cst_ref[pl.ds(h0, SSM_HPG), :]
        cs_g = _dot_parts(cs_parts, sel_ref[j])
        cs_x = _dot_parts(cs_parts, esel_ref[j])
        dt_x = _dot_parts(dt_parts, esel_ref[j])

        xdt = xs * dt_x
        xb = xdt.astype(BF16)
        bb = bm.astype(BF16)
        cb = cm.astype(BF16)
        cbm = lax.dot_general(cb, bb, (((1,), (1,)), ((), ())), preferred_element_type=F32)

        yd = []
        for blk in range(2):
            xblk = xb[:, blk * half:(blk + 1) * half]
            acc = jnp.zeros((L, half), F32)
            for e4 in range(SSM_HPG // 2):
                e = blk * (SSM_HPG // 2) + e4
                diff = cs_g[:, e:e + 1] - cs_r[e:e + 1, :]
                lm = jnp.exp(jnp.where(causal, diff, -jnp.inf))
                m = (cbm * lm).astype(BF16)
                xm = jnp.where(lane_head == e4, xblk, jnp.zeros_like(xblk))
                acc = acc + _dot(m, xm)
            yd.append(acc)
        y = jnp.concatenate(yd, axis=1)

        hprev = h_ref[j]
        y = y + _dot(cb, hprev.astype(BF16)) * jnp.exp(cs_x)
        cs_last = cs_x[L - 1:L, :]
        xdec = (xdt * jnp.exp(cs_last - cs_x)).astype(BF16)
        states = lax.dot_general(bb, xdec, (((0,), (0,)), ((), ())), preferred_element_type=F32)
        h_ref[j] = hprev * jnp.exp(cs_last) + states

        y = y + xs * dsk_ref[:, wl:wh]
        y = y * _silu(z_ref[:, wl:wh].astype(F32))
        ms = jnp.mean(y * y, axis=-1, keepdims=True)
        o_ref[:, wl:wh] = (y * lax.rsqrt(ms + EPS) * ng_ref[:, wl:wh]).astype(o_ref.dtype)


def _ssd_call(zx, dt, alog, conv_w, conv_b, dskip, norm_g):
    S = zx.shape[0]
    P = SSD_GPS
    G, L, W, N = SSM_GROUPS, CHUNK, P * GROUP_W, P * SSM_STATE
    nc = S // L
    xo = D_INNER // W
    bo = 2 * D_INNER // N
    co = bo + G // P
    hb = L // HALO

    def halo(off):
        return lambda g, c: (jnp.maximum(c * hb - 1, 0), off + g)

    cw = conv_w
    cb = conv_b.reshape(1, -1)
    cxo, cbo, cco = 0, D_INNER // N, D_INNER // N + G // P
    in_specs = [
        pl.BlockSpec((L, W), lambda g, c: (c, g)),
        pl.BlockSpec((L, W), lambda g, c: (c, xo + g)),
        pl.BlockSpec((HALO, W), halo(xo)),
        pl.BlockSpec((L, N), lambda g, c: (c, bo + g)),
        pl.BlockSpec((HALO, N), halo(bo)),
        pl.BlockSpec((L, N), lambda g, c: (c, co + g)),
        pl.BlockSpec((HALO, N), halo(co)),
        pl.BlockSpec((L, LANES), lambda g, c: (c, 0)),
        pl.BlockSpec((1, LANES), lambda g, c: (0, 0)),
        pl.BlockSpec((CONV_WIDTH, W), lambda g, c: (0, cxo + g)),
        pl.BlockSpec((1, W), lambda g, c: (0, cxo + g)),
        pl.BlockSpec((CONV_WIDTH, N), lambda g, c: (0, cbo + g)),
        pl.BlockSpec((1, N), lambda g, c: (0, cbo + g)),
        pl.BlockSpec((CONV_WIDTH, N), lambda g, c: (0, cco + g)),
        pl.BlockSpec((1, N), lambda g, c: (0, cco + g)),
        pl.BlockSpec((1, W), lambda g, c: (0, g)),
        pl.BlockSpec((1, W), lambda g, c: (0, g)),
    ]
    return pl.pallas_call(
        _ssd_kernel,
        grid=(G // P, nc),
        in_specs=in_specs,
        out_specs=pl.BlockSpec((L, W), lambda g, c: (c, g)),
        out_shape=jax.ShapeDtypeStruct((S, D_INNER), BF16),
        scratch_shapes=[pltpu.VMEM((P, SSM_STATE, GROUP_W), F32), pltpu.VMEM((LANES, L), F32),
                        pltpu.VMEM((CONV_WIDTH, L, L), BF16), pltpu.VMEM((P, LANES, LANES), BF16),
                        pltpu.VMEM((P, LANES, GROUP_W), BF16)],
        compiler_params=_cparams("parallel", "arbitrary"),
        name="ssd_chunk_scan",
    )(zx, zx, zx, zx, zx, zx, zx, dt, alog, cw, cb, cw, cb, cw, cb, dskip, norm_g)


def _attn_kernel(q_ref, k_ref, v_ref, lq1_ref, lk1_ref, lq2_ref, lk2_ref, sg_ref,
                 o_ref, m_ref, l_ref, acc_ref, p_ref, alpha_ref, *, lambda_init):
    qi = pl.program_id(1)
    T = ATT_T
    HD = DA_HEAD_DIM
    m_ref[...] = jnp.full_like(m_ref, -jnp.inf)
    l_ref[...] = jnp.zeros_like(l_ref)
    acc_ref[...] = jnp.zeros_like(acc_ref)

    def softmax_slab(kj, idx, masked):
        r0 = pl.multiple_of(kj * T, T)
        t = idx % 2
        q = q_ref[:, idx * HD:(idx + 1) * HD]
        kt = k_ref[pl.ds(r0, T), t * HD:(t + 1) * HD]
        s = lax.dot_general(q, kt, (((1,), (1,)), ((), ())), preferred_element_type=F32)
        if masked:
            row = lax.broadcasted_iota(jnp.int32, (T, T), 0)
            col = lax.broadcasted_iota(jnp.int32, (T, T), 1)
            s = jnp.where(col <= row, s, -jnp.inf)
        m_prev = m_ref[idx]
        m_new = jnp.maximum(m_prev, jnp.max(s, axis=-1, keepdims=True))
        alpha_ref[idx] = jnp.exp2(m_prev - m_new)
        p = jnp.exp2(s - _lane_tile(m_new, T // LANES))
        psum = p[:, 0:LANES]
        for b in range(1, T // LANES):
            psum = psum + p[:, b * LANES:(b + 1) * LANES]
        l_ref[idx] = alpha_ref[idx] * l_ref[idx] + psum
        p_ref[idx] = p.astype(BF16)
        m_ref[idx] = m_new

    def pv_slab(kj, idx):
        r0 = pl.multiple_of(kj * T, T)
        v = v_ref[pl.ds(r0, T), :]
        acc_ref[idx] = _lane_tile(alpha_ref[idx], DA_V_DIM // LANES) * acc_ref[idx] + _dot(p_ref[idx], v)

    n_slab = 2 * DA_GROUP
    for idx in range(n_slab):
        softmax_slab(qi, idx, True)

    def body(kj, prev):
        for idx in range(n_slab):
            pv_slab(prev, idx)
            softmax_slab(kj, idx, False)
        return kj

    last = lax.fori_loop(0, qi, body, qi)
    for idx in range(n_slab):
        pv_slab(last, idx)

    lam = (jnp.exp(jnp.sum(lq1_ref[...] * lk1_ref[...], axis=-1, keepdims=True))
           - jnp.exp(jnp.sum(lq2_ref[...] * lk2_ref[...], axis=-1, keepdims=True)) + lambda_init)
    for gq in range(DA_GROUP):
        a0 = acc_ref[gq * 2] / jnp.sum(l_ref[gq * 2], axis=-1, keepdims=True)
        a1 = acc_ref[gq * 2 + 1] / jnp.sum(l_ref[gq * 2 + 1], axis=-1, keepdims=True)
        o = a0 - lam * a1
        ms = jnp.mean(o * o, axis=-1, keepdims=True)
        o = (o * lax.rsqrt(ms + EPS) * sg_ref[...]) * (1.0 - lambda_init)
        o_ref[:, gq * DA_V_DIM:(gq + 1) * DA_V_DIM] = o.astype(o_ref.dtype)


def _attn_call(q, kv, lq1, lk1, lq2, lk2, subln_g, lambda_init):
    S = q.shape[0]
    v_off = K_DIM // DA_V_DIM
    T = ATT_T
    qw = DA_GROUP * 2 * DA_HEAD_DIM
    kw = 2 * DA_HEAD_DIM
    vec = pl.BlockSpec((1, DA_HEAD_DIM), lambda h, i: (0, 0))
    return pl.pallas_call(
        functools.partial(_attn_kernel, lambda_init=lambda_init),
        grid=(DA_KV_HEADS, S // T),
        in_specs=[pl.BlockSpec((T, qw), lambda h, i: (i, h)),
                  pl.BlockSpec((S, kw), lambda h, i: (0, h)),
                  pl.BlockSpec((S, DA_V_DIM), lambda h, i: (0, v_off + h)),
                  vec, vec, vec, vec,
                  pl.BlockSpec((1, DA_V_DIM), lambda h, i: (0, 0))],
        out_specs=pl.BlockSpec((T, DA_GROUP * DA_V_DIM), lambda h, i: (i, h)),
        out_shape=jax.ShapeDtypeStruct((S, DA_HEADS * DA_V_DIM), BF16),
        scratch_shapes=[pltpu.VMEM((2 * DA_GROUP, T, LANES), F32),
                        pltpu.VMEM((2 * DA_GROUP, T, LANES), F32),
                        pltpu.VMEM((2 * DA_GROUP, T, DA_V_DIM), F32),
                        pltpu.VMEM((2 * DA_GROUP, T, T), BF16),
                        pltpu.VMEM((2 * DA_GROUP, T, LANES), F32)],
        compiler_params=_cparams("parallel", "parallel"),
        name="diff_attention",
    )(q, kv, kv, lq1, lk1, lq2, lk2, subln_g)


def _router_kernel(x_ref, g_ref, sh_ref, sc_ref, rw_ref, rb_ref, h_ref, idx_ref, wt_ref, cnt_ref, carry_ref,
                   *, n_real):
    i = pl.program_id(0)

    @pl.when(i == 0)
    def _():
        carry_ref[...] = jnp.zeros_like(carry_ref)

    @pl.when(i >= n_real)
    def _():
        h_ref[...] = jnp.zeros_like(h_ref)

    @pl.when(i < n_real)
    def _():
        _route(x_ref, g_ref, sh_ref, sc_ref, rw_ref, rb_ref, h_ref, idx_ref, wt_ref, cnt_ref, carry_ref)


def _route(x_ref, g_ref, sh_ref, sc_ref, rw_ref, rb_ref, h_ref, idx_ref, wt_ref, cnt_ref, carry_ref):
    h = _norm_mod(x_ref[...], g_ref[...], sh_ref[...], sc_ref[...])
    h_ref[...] = h.astype(BF16)
    hh = h.astype(BF16)
    hl = (h - hh.astype(F32)).astype(BF16)
    rw = rw_ref[...]
    wh = rw.astype(BF16)
    wl = (rw - wh.astype(F32)).astype(BF16)
    logits = _dot(hh, wh) + _dot(hl, wh) + _dot(hh, wl)
    lt = logits.T[0:N_EXPERTS, :]
    score = _sigmoid(lt)
    sel = score + rb_ref[...]
    srow = [score[e:e + 1, :] for e in range(N_EXPERTS)]
    row = [sel[e:e + 1, :] for e in range(N_EXPERTS)]

    gscore = []
    for gi in range(N_EXPERT_GROUPS):
        r = row[gi * EXPERTS_PER_GROUP:(gi + 1) * EXPERTS_PER_GROUP]
        best = None
        for a in range(EXPERTS_PER_GROUP):
            for b in range(a + 1, EXPERTS_PER_GROUP):
                s = r[a] + r[b]
                best = s if best is None else jnp.maximum(best, s)
        gscore.append(best)
    bestg = jnp.zeros_like(gscore[0], dtype=jnp.int32)
    bestv = gscore[0]
    for gi in range(1, N_EXPERT_GROUPS):
        upd = gscore[gi] > bestv
        bestg = jnp.where(upd, gi, bestg)
        bestv = jnp.where(upd, gscore[gi], bestv)

    cand = [jnp.where(bestg == e // EXPERTS_PER_GROUP, row[e], -jnp.inf) for e in range(N_EXPERTS)]

    def first_argmax(vals):
        bi = jnp.zeros_like(bestg)
        bv = vals[0]
        for e in range(1, N_EXPERTS):
            upd = vals[e] > bv
            bi = jnp.where(upd, e, bi)
            bv = jnp.where(upd, vals[e], bv)
        return bi

    i1 = first_argmax(cand)
    i2 = first_argmax([jnp.where(i1 == e, -jnp.inf, cand[e]) for e in range(N_EXPERTS)])
    s1 = sum(jnp.where(i1 == e, srow[e], 0.0) for e in range(N_EXPERTS))
    s2 = sum(jnp.where(i2 == e, srow[e], 0.0) for e in range(N_EXPERTS))
    den = s1 + s2
    tm = i1.shape[1]
    e_id = lax.broadcasted_iota(jnp.int32, (LANES, tm), 0)
    oh1 = jnp.where(e_id == i1, 1.0, 0.0)
    oh2 = jnp.where(e_id == i2, 1.0, 0.0)
    oh = (oh1 + oh2).T
    r_t = lax.broadcasted_iota(jnp.int32, (tm, tm), 0)
    c_t = lax.broadcasted_iota(jnp.int32, (tm, tm), 1)
    tril = jnp.where(r_t >= c_t, 1.0, 0.0).astype(BF16)
    incl = _dot(tril, oh.astype(BF16))
    carry = carry_ref[0:1, :]
    base = (incl - oh + carry).T
    rank1 = jnp.sum(oh1 * base, axis=0, keepdims=True).astype(jnp.int32)
    rank2 = jnp.sum(oh2 * base, axis=0, keepdims=True).astype(jnp.int32)
    total = carry + incl[tm - 1:tm, :]
    carry_ref[...] = jnp.broadcast_to(total, carry_ref.shape)
    cnt_ref[...] = jnp.broadcast_to(total, cnt_ref.shape).astype(jnp.int32)
    idx_ref[...] = jnp.concatenate([i1, i2, rank1, rank2, jnp.zeros((4, tm), jnp.int32)], axis=0)
    wt_ref[...] = jnp.concatenate([s1 / den, s2 / den, jnp.zeros((LANES - 2, tm), F32)], axis=0).T


def _router_call(x, g, sh, sc, rw_pad, rb_col, h_rows):
    S, D = x.shape
    tm = ROUTER_TM
    n_real = S // tm
    last = n_real - 1
    vec = pl.BlockSpec((1, D), lambda i: (0, 0))
    return pl.pallas_call(
        functools.partial(_router_kernel, n_real=n_real),
        grid=(h_rows // tm,),
        in_specs=[pl.BlockSpec((tm, D), lambda i: (jnp.minimum(i, last), 0)), vec, vec, vec,
                  pl.BlockSpec((D, LANES), lambda i: (0, 0)),
                  pl.BlockSpec((N_EXPERTS, 1), lambda i: (0, 0))],
        out_specs=[pl.BlockSpec((tm, D), lambda i: (i, 0)),
                   pl.BlockSpec((8, tm), lambda i: (0, jnp.minimum(i, last))),
                   pl.BlockSpec((tm, LANES), lambda i: (jnp.minimum(i, last), 0)),
                   pl.BlockSpec((8, LANES), lambda i: (0, 0))],
        out_shape=[jax.ShapeDtypeStruct((h_rows, D), BF16),
                   jax.ShapeDtypeStruct((8, S), jnp.int32),
                   jax.ShapeDtypeStruct((S, LANES), F32),
                   jax.ShapeDtypeStruct((8, LANES), jnp.int32)],
        scratch_shapes=[pltpu.VMEM((8, LANES), F32)],
        compiler_params=_cparams("arbitrary"),
        name="moe_norm_router",
    )(x, g, sh, sc, rw_pad, rb_col)


def _moe_kernel(ti_ref, te_ref, tf_ref, nv_ref, x_ref, wg_ref, wu_ref, wd_ref, o_ref,
                wgb_ref, wub_ref, wdb_ref):
    t = pl.program_id(0)

    @pl.when(t < nv_ref[0])
    def _():
        @pl.when(tf_ref[t] == 1)
        def _():
            wgb_ref[...] = wg_ref[...].astype(BF16)
            wub_ref[...] = wu_ref[...].astype(BF16)
            wdb_ref[...] = wd_ref[...].astype(BF16)

        x = x_ref[...]
        gt = _dot(x, wgb_ref[...])
        up = _dot(x, wub_ref[...])
        h = _silu(gt) * up
        o_ref[...] = _dot(h.astype(BF16), wdb_ref[...]).astype(o_ref.dtype)

    @pl.when(t >= nv_ref[0])
    def _():
        o_ref[...] = jnp.zeros_like(o_ref)


def _moe_call(xs, tile_idx, tile_expert, tile_first, n_valid, w_gate, w_up, w_down, layer):
    R, D = xs.shape
    tm = MOE_TM
    F = D_EXPERT
    grid_spec = pltpu.PrefetchScalarGridSpec(
        num_scalar_prefetch=4,
        grid=(R // tm,),
        in_specs=[pl.BlockSpec((tm, D), lambda t, ti, te, tf, nv: (ti[t], 0)),
                  pl.BlockSpec((None, None, D, F), lambda t, ti, te, tf, nv: (layer, te[t], 0, 0)),
                  pl.BlockSpec((None, None, D, F), lambda t, ti, te, tf, nv: (layer, te[t], 0, 0)),
                  pl.BlockSpec((None, None, F, D), lambda t, ti, te, tf, nv: (layer, te[t], 0, 0))],
        out_specs=pl.BlockSpec((tm, D), lambda t, ti, te, tf, nv: (t, 0)),
        scratch_shapes=[pltpu.VMEM((D, F), BF16), pltpu.VMEM((D, F), BF16), pltpu.VMEM((F, D), BF16)],
    )
    return pl.pallas_call(
        _moe_kernel,
        grid_spec=grid_spec,
        out_shape=jax.ShapeDtypeStruct((R, D), BF16),
        compiler_params=_cparams("arbitrary"),
        name="moe_experts",
    )(tile_idx, tile_expert, tile_first, n_valid, xs, w_gate, w_up, w_down)


def _combine_kernel(x_ref, y1_ref, y2_ref, wt_ref, g_ref, fg_ref, o_ref, *, final):
    wt = wt_ref[...]
    y = wt[:, 0:1] * y1_ref[...].astype(F32) + wt[:, 1:2] * y2_ref[...].astype(F32)
    x = x_ref[...] + g_ref[...] * y
    if final:
        ms = jnp.mean(x * x, axis=-1, keepdims=True)
        x = x * lax.rsqrt(ms + EPS) * fg_ref[...]
    o_ref[...] = x


def _combine_call(x, y1, y2, wts, gate, final_g, final, part):
    S, D = x.shape
    tm = COMBINE_TM
    off = part * (y1.shape[0] // tm)
    blk_x = pl.BlockSpec((tm, D), lambda i: (i + off, 0))
    blk_y = pl.BlockSpec((tm, D), lambda i: (i, 0))
    vec = pl.BlockSpec((1, D), lambda i: (0, 0))
    return pl.pallas_call(
        functools.partial(_combine_kernel, final=final),
        grid=(y1.shape[0] // tm,),
        in_specs=[blk_x, blk_y, blk_y, pl.BlockSpec((tm, LANES), lambda i: (i + off, 0)), vec, vec],
        out_specs=blk_x,
        out_shape=jax.ShapeDtypeStruct((S, D), F32),
        input_output_aliases={0: 0},
        compiler_params=_cparams("parallel"),
        name="moe_combine",
    )(x, y1, y2, wts, gate, final_g)


def _take_rows(a, rows):
    return a.at[rows].get(mode="promise_in_bounds")


def _moe_rows(S):
    return TOP_K * S + N_EXPERTS * MOE_TM


def _dispatch_plan(idx, counts, S):
    tm = MOE_TM
    R = _moe_rows(S)
    n_tiles = R // tm
    e_flat = idx[:TOP_K].reshape(-1)
    rank = idx[TOP_K:2 * TOP_K].reshape(-1)
    tok_flat = jnp.tile(jnp.arange(S, dtype=jnp.int32), TOP_K)
    onehot = (e_flat[:, None] == jnp.arange(N_EXPERTS, dtype=jnp.int32)[None, :]).astype(jnp.int32)
    tiles_e = (counts + tm - 1) // tm
    tile_end = jnp.cumsum(tiles_e)
    tile_start = tile_end - tiles_e
    dest = jnp.sum(onehot * (tile_start * tm)[None, :], axis=1) + rank
    src_tok = (jnp.arange(R, dtype=jnp.int32) % S).at[dest].set(
        tok_flat, mode="promise_in_bounds", unique_indices=True)
    n_valid = tile_end[-1]
    t_ids = jnp.minimum(jnp.arange(n_tiles, dtype=jnp.int32), n_valid - 1)
    tile_expert = jnp.sum((t_ids[:, None] >= tile_end[None, :]).astype(jnp.int32), axis=1)
    tile_first = jnp.sum(((t_ids[:, None] == tile_start[None, :]) & (tiles_e[None, :] > 0)).astype(jnp.int32),
                         axis=1)
    return (src_tok, t_ids, tile_expert, tile_first, n_valid.reshape(1).astype(jnp.int32), dest[:S], dest[S:])


def _moe_block(x, g, sh, sc, gate, rw_pad, rb_col, w_gate, w_up, w_down, layer, final_g, final):
    S = x.shape[0]
    h, idx, wts, cnt = _router_call(x, g, sh, sc, rw_pad, rb_col, _moe_rows(S) + ROUTER_TM)
    src_tok, t_ids, t_exp, t_first, n_valid, pos1, pos2 = _dispatch_plan(idx, cnt[0, :N_EXPERTS], S)
    xs = _take_rows(h, src_tok)
    ys = _moe_call(xs, t_ids, t_exp, t_first, n_valid, w_gate, w_up, w_down, layer)
    n = S // COMBINE_PARTS
    for part in range(COMBINE_PARTS):
        y1 = _take_rows(ys, pos1[part * n:(part + 1) * n])
        y2 = _take_rows(ys, pos2[part * n:(part + 1) * n])
        x = _combine_call(x, y1, y2, wts, gate, final_g, final, part)
    return x


def _rope_tables(S):
    inv = 1.0 / (ROPE_THETA ** (jnp.arange(0, ROPE_DIM, 2, dtype=F32) / ROPE_DIM))
    ang = jnp.arange(S, dtype=F32)[:, None] * inv[None, :]
    cos, sin = jnp.cos(ang), jnp.sin(ang)
    zeros = jnp.zeros((S, LANES - ROPE_DIM), F32)
    zh = jnp.zeros((S, ROPE_HALF), F32)
    cos_t = jnp.concatenate([cos, cos, jnp.ones((S, LANES - ROPE_DIM), F32)], axis=1)
    sin_a = jnp.concatenate([-sin, zh, zeros], axis=1)
    sin_b = jnp.concatenate([zh, sin, zeros], axis=1)
    return cos_t, sin_a, sin_b


def kernel(x, c, mod_w, mod_b, norm1_g, norm2_g, ssm_in_w, ssm_conv_w, ssm_conv_b, ssm_dt_bias, ssm_a_log, ssm_d, ssm_norm_g, ssm_out_w, kv_mod_w, kv_mod_b, kv_norm_g, w_kv, attn_q_w, lam_q1, lam_k1, lam_q2, lam_k2, subln_g, attn_o_w, moe_w_gate, moe_w_up, moe_w_down, router_w, router_b, final_g):
    _, S, D = x.shape
    xr = x[0]
    cb = jnp.broadcast_to(c[0][:, None], (D, LANES))
    mods = _mod_call(cb, mod_w, mod_b)
    kvm = _mod_call(cb, kv_mod_w[None], kv_mod_b[None])
    cos_t, sin_a, sin_b = _rope_tables(S)
    rw_pad = jnp.pad(router_w, ((0, 0), (0, LANES - N_EXPERTS)))
    rb_col = router_b.reshape(N_EXPERTS, 1)
    row = lambda v: v.reshape(1, -1)
    pad_l = lambda v: jnp.pad(v.reshape(1, -1), ((0, 0), (0, LANES - v.shape[-1])))

    kv = None
    for layer in range(DEPTH):
        m = mods[layer]
        sh1, sc1, g1, sh2, sc2, g2 = [m[:, i * D:(i + 1) * D] for i in range(N_MOD)]
        if layer == N_A_LAYERS:
            kv_sh, kv_sc = kvm[0][:, :D], kvm[0][:, D:]
            kv = _rope_proj_call(xr, row(kv_norm_g), kv_sh, kv_sc, w_kv.astype(BF16), cos_t, sin_a, sin_b,
                                 rope_cols=K_DIM, scale=1.0, name="kv_proj")
        if layer < N_A_LAYERS:
            i = layer
            w_in = ssm_in_w[i]
            zx, dt = _in_proj_call(xr, row(norm1_g[layer]), sh1, sc1,
                                   w_in.astype(BF16),
                                   jnp.pad(w_in[:, ZX_DIM:], ((0, 0), (0, LANES - SSM_HEADS))).astype(BF16),
                                   pad_l(ssm_dt_bias[i]))
            y = _ssd_call(zx, dt, pad_l(ssm_a_log[i]), ssm_conv_w[i], ssm_conv_b[i],
                          row(jnp.repeat(ssm_d[i], SSM_HEAD_DIM)), row(ssm_norm_g[i]))
            xr = _res_mm_call(y, ssm_out_w[i].astype(BF16), xr, g1, name="ssm_out_proj")
        else:
            j = layer - N_A_LAYERS
            lambda_init = 0.8 - 0.6 * float(np.exp(-0.3 * layer))
            q = _rope_proj_call(xr, row(norm1_g[layer]), sh1, sc1, attn_q_w[j].astype(BF16),
                                cos_t, sin_a, sin_b, rope_cols=D, scale=DA_HEAD_DIM ** -0.5 * LOG2E,
                                name="q_proj")
            o = _attn_call(q, kv, row(lam_q1[j]), row(lam_k1[j]), row(lam_q2[j]), row(lam_k2[j]),
                           row(subln_g[j]), lambda_init)
            xr = _res_mm_call(o, attn_o_w[j].astype(BF16), xr, g1, name="attn_o_proj")
        xr = _moe_block(xr, row(norm2_g[layer]), sh2, sc2, g2, rw_pad, rb_col,
                        moe_w_gate, moe_w_up, moe_w_down, layer, row(final_g), layer == DEPTH - 1)
    return xr[None]
```

```python
import functools

import numpy as np
import jax
import jax.numpy as jnp
from jax import lax
from jax.experimental import pallas as pl
from jax.experimental.pallas import tpu as pltpu

F32 = jnp.float32
BF16 = jnp.bfloat16

D_MODEL = 2048
DEPTH = 4
N_A_LAYERS = DEPTH // 2
EPS = 1e-5
N_MOD = 6

D_INNER = 2 * D_MODEL
SSM_HEAD_DIM = 64
SSM_HEADS = D_INNER // SSM_HEAD_DIM
SSM_GROUPS = 8
SSM_HPG = SSM_HEADS // SSM_GROUPS
SSM_STATE = 128
CONV_WIDTH = 4
CHUNK = 256
GROUP_W = SSM_HPG * SSM_HEAD_DIM
ZX_DIM = 2 * D_INNER + 2 * SSM_GROUPS * SSM_STATE

DA_HEADS = 8
DA_KV_HEADS = 4
DA_GROUP = DA_HEADS // DA_KV_HEADS
DA_HEAD_DIM = D_MODEL // DA_HEADS // 2
DA_V_DIM = 2 * DA_HEAD_DIM
K_DIM = DA_KV_HEADS * 2 * DA_HEAD_DIM
V_DIM = DA_KV_HEADS * DA_V_DIM
ROPE_DIM = DA_HEAD_DIM // 4
ROPE_HALF = ROPE_DIM // 2
ROPE_THETA = 500000.0

N_EXPERTS = 16
N_EXPERT_GROUPS = 4
EXPERTS_PER_GROUP = N_EXPERTS // N_EXPERT_GROUPS
TOP_K = 2
D_EXPERT = 512

LOG2E = 1.4426950408889634
LANES = 128
HALO = 16
VMEM_LIMIT = 48 * 1024 * 1024

PROJ_TM = 1024
PROJ_TN = 512
WIDE_TN = 1024
ATT_T = 512
MOE_TM = 256
SSD_GPS = 4
ROUTER_TM = 512
COMBINE_TM = 512
COMBINE_PARTS = 2


def _cparams(*sem):
    return pltpu.CompilerParams(dimension_semantics=sem, vmem_limit_bytes=VMEM_LIMIT)


def _sigmoid(v):
    return 1.0 / (1.0 + jnp.exp(-v))


def _silu(v):
    hv = 0.5 * v
    return hv * (1.0 + jnp.tanh(hv))


def _lane_tile(v, n):
    return v if n == 1 else jnp.concatenate([v] * n, axis=1)


def _norm_mod(x, g, sh, sc):
    ms = jnp.mean(x * x, axis=-1, keepdims=True)
    return (x * lax.rsqrt(ms + EPS) * g) * (1.0 + sc) + sh


def _split3(v):
    hi = v.astype(BF16)
    r1 = v - hi.astype(F32)
    mid = r1.astype(BF16)
    lo = (r1 - mid.astype(F32)).astype(BF16)
    return hi, mid, lo


def _dot(a, b):
    return jnp.dot(a, b, preferred_element_type=F32)


def _dot_parts(parts, sel):
    acc = _dot(parts[0], sel)
    for p in parts[1:]:
        acc = acc + _dot(p, sel)
    return acc


def _dot3_l(sel, v):
    hi, mid, lo = _split3(v)
    return _dot(sel, hi) + _dot(sel, mid) + _dot(sel, lo)


def _mod_kernel(c_ref, w_ref, b_ref, o_ref):
    cc = c_ref[...]
    sc = _silu(cc)
    w = w_ref[...]
    o_ref[...] = jnp.sum(w * _lane_tile(sc, w.shape[1] // LANES), axis=0, keepdims=True) + b_ref[...]


def _mod_call(cb, w, b):
    L, D, N = w.shape
    tn = 1024
    return pl.pallas_call(
        _mod_kernel,
        grid=(L, N // tn),
        in_specs=[pl.BlockSpec((D, LANES), lambda l, j: (0, 0)),
                  pl.BlockSpec((None, D, tn), lambda l, j: (l, 0, j)),
                  pl.BlockSpec((None, 1, tn), lambda l, j: (l, 0, j))],
        out_specs=pl.BlockSpec((None, 1, tn), lambda l, j: (l, 0, j)),
        out_shape=jax.ShapeDtypeStruct((L, 1, N), F32),
        compiler_params=_cparams("parallel", "parallel"),
        name="mod_vectors",
    )(cb, w, b.reshape(L, 1, N))


def _norm_prologue(x_ref, g_ref, sh_ref, sc_ref, h_ref):
    tm = x_ref.shape[0]
    rc = 256
    for r in range(0, tm, rc):
        h = _norm_mod(x_ref[r:r + rc, :], g_ref[...], sh_ref[...], sc_ref[...])
        h_ref[r:r + rc, :] = h.astype(BF16)


def _in_proj_kernel(x_ref, g_ref, sh_ref, sc_ref, w_ref, wdt_ref, dtb_ref, o_ref, dt_ref, h_ref):
    @pl.when(pl.program_id(1) == 0)
    def _():
        _norm_prologue(x_ref, g_ref, sh_ref, sc_ref, h_ref)
        d = _dot(h_ref[...], wdt_ref[...]) + dtb_ref[...]
        dt_ref[...] = jnp.maximum(d, 0.0) + jnp.log1p(jnp.exp(-jnp.abs(d)))

    o_ref[...] = _dot(h_ref[...], w_ref[...]).astype(o_ref.dtype)


def _in_proj_call(x, g, sh, sc, w, wdt, dtb):
    M, K = x.shape
    N = ZX_DIM
    tm, tn = PROJ_TM, WIDE_TN
    vec = pl.BlockSpec((1, K), lambda i, j: (0, 0))
    return pl.pallas_call(
        _in_proj_kernel,
        grid=(M // tm, N // tn),
        in_specs=[pl.BlockSpec((tm, K), lambda i, j: (i, 0)), vec, vec, vec,
                  pl.BlockSpec((K, tn), lambda i, j: (0, j)),
                  pl.BlockSpec((K, LANES), lambda i, j: (0, 0)),
                  pl.BlockSpec((1, LANES), lambda i, j: (0, 0))],
        out_specs=[pl.BlockSpec((tm, tn), lambda i, j: (i, j)),
                   pl.BlockSpec((tm, LANES), lambda i, j: (i, 0))],
        out_shape=[jax.ShapeDtypeStruct((M, N), BF16), jax.ShapeDtypeStruct((M, LANES), F32)],
        scratch_shapes=[pltpu.VMEM((tm, K), BF16)],
        compiler_params=_cparams("parallel", "arbitrary"),
        name="ssm_in_proj",
    )(x, g, sh, sc, w, wdt, dtb)


def _rope_proj_kernel(x_ref, g_ref, sh_ref, sc_ref, w_ref, cos_ref, sa_ref, sb_ref, o_ref, h_ref,
                      *, rope_tiles, n_tiles, scale):
    j = pl.program_id(1)

    @pl.when(j == 0)
    def _():
        _norm_prologue(x_ref, g_ref, sh_ref, sc_ref, h_ref)

    acc = _dot(h_ref[...], w_ref[...])
    tn = acc.shape[1]
    nb = tn // LANES

    def roped():
        cos = _lane_tile(cos_ref[...], nb)
        sa = _lane_tile(sa_ref[...], nb)
        sb = _lane_tile(sb_ref[...], nb)
        r = acc * cos + pltpu.roll(acc, tn - ROPE_HALF, 1) * sa + pltpu.roll(acc, ROPE_HALF, 1) * sb
        return (r * scale).astype(o_ref.dtype)

    if rope_tiles >= n_tiles:
        o_ref[...] = roped()
    else:
        @pl.when(j < rope_tiles)
        def _():
            o_ref[...] = roped()

        @pl.when(j >= rope_tiles)
        def _():
            o_ref[...] = acc.astype(o_ref.dtype)


def _rope_proj_call(x, g, sh, sc, w, cos, sa, sb, *, rope_cols, scale, name):
    M, K = x.shape
    N = w.shape[1]
    tm, tn = PROJ_TM, WIDE_TN
    vec = pl.BlockSpec((1, K), lambda i, j: (0, 0))
    tab = pl.BlockSpec((tm, LANES), lambda i, j: (i, 0))
    kern = functools.partial(_rope_proj_kernel, rope_tiles=rope_cols // tn, n_tiles=N // tn, scale=scale)
    return pl.pallas_call(
        kern,
        grid=(M // tm, N // tn),
        in_specs=[pl.BlockSpec((tm, K), lambda i, j: (i, 0)), vec, vec, vec,
                  pl.BlockSpec((K, tn), lambda i, j: (0, j)), tab, tab, tab],
        out_specs=pl.BlockSpec((tm, tn), lambda i, j: (i, j)),
        out_shape=jax.ShapeDtypeStruct((M, N), BF16),
        scratch_shapes=[pltpu.VMEM((tm, K), BF16)],
        compiler_params=_cparams("parallel", "arbitrary"),
        name=name,
    )(x, g, sh, sc, w, cos, sa, sb)


def _res_mm_kernel(a_ref, w_ref, res_ref, gate_ref, o_ref):
    o_ref[...] = res_ref[...] + gate_ref[...] * _dot(a_ref[...], w_ref[...])


def _res_mm_call(a, w, res, gate, *, name):
    M, K = a.shape
    N = w.shape[1]
    tm, tn = PROJ_TM, (WIDE_TN if K <= D_MODEL else PROJ_TN)
    return pl.pallas_call(
        _res_mm_kernel,
        grid=(M // tm, N // tn),
        in_specs=[pl.BlockSpec((tm, K), lambda i, j: (i, 0)),
                  pl.BlockSpec((K, tn), lambda i, j: (0, j)),
                  pl.BlockSpec((tm, tn), lambda i, j: (i, j)),
                  pl.BlockSpec((1, tn), lambda i, j: (0, j))],
        out_specs=pl.BlockSpec((tm, tn), lambda i, j: (i, j)),
        out_shape=jax.ShapeDtypeStruct((M, N), F32),
        compiler_params=_cparams("parallel", "parallel"),
        name=name,
    )(a, w, res, gate)


def _ssd_kernel(z_ref, x_ref, xh_ref, b_ref, bh_ref, c_ref, ch_ref, dt_ref, alog_ref,
                cwx_ref, cbx_ref, cwb_ref, cbb_ref, cwc_ref, cbc_ref, dsk_ref, ng_ref,
                o_ref, h_ref, cst_ref, shift_ref, sel_ref, esel_ref):
    gp = pl.program_id(0)
    c = pl.program_id(1)
    L, W, N = CHUNK, GROUP_W, SSM_STATE
    row = lax.broadcasted_iota(jnp.int32, (L, L), 0)
    col = lax.broadcasted_iota(jnp.int32, (L, L), 1)
    causal = row >= col

    @pl.when(c == 0)
    def _():
        h_ref[...] = jnp.zeros_like(h_ref)
        shift_ref[0] = jnp.where(causal, 1.0, 0.0).astype(BF16)
        for k in range(1, CONV_WIDTH):
            shift_ref[k] = jnp.where(row - col == k, 1.0, 0.0).astype(BF16)
        k1 = lax.broadcasted_iota(jnp.int32, (LANES, LANES), 0)
        n1 = lax.broadcasted_iota(jnp.int32, (LANES, LANES), 1)
        k2 = lax.broadcasted_iota(jnp.int32, (LANES, W), 0)
        n2 = lax.broadcasted_iota(jnp.int32, (LANES, W), 1) // SSM_HEAD_DIM
        for j in range(SSD_GPS):
            h0 = (gp * SSD_GPS + j) * SSM_HPG
            sel_ref[j] = jnp.where(jnp.where(n1 < SSM_HPG, k1 - n1, -1) == h0, 1.0, 0.0).astype(BF16)
            esel_ref[j] = jnp.where(k2 - n2 == h0, 1.0, 0.0).astype(BF16)

    def conv_silu(cur_b, halo_b, w_ref, bias, lo, hi):
        acc = bias + w_ref[CONV_WIDTH - 1:CONV_WIDTH, lo:hi] * cur_b.astype(F32)
        prev8 = jnp.where(c == 0, 0.0, halo_b.astype(F32))[HALO - 8:HALO, :]
        row8 = lax.broadcasted_iota(jnp.int32, prev8.shape, 0)
        top = jnp.zeros_like(prev8)
        for k in range(1, CONV_WIDTH):
            wk = w_ref[CONV_WIDTH - 1 - k:CONV_WIDTH - k, lo:hi]
            acc = acc + wk * _dot(shift_ref[k], cur_b)
            top = top + wk * jnp.where(row8 < k, pltpu.roll(prev8, k, 0), 0.0)
        acc = jnp.concatenate([acc[0:8, :] + top, acc[8:, :]], axis=0)
        return _silu(acc)

    dt_all = dt_ref[...]
    dta = dt_all * (-jnp.exp(alog_ref[...]))
    cs_all = _dot3_l(shift_ref[0], dta)
    cst_ref[...] = cs_all.T
    cs_parts = _split3(cs_all)
    dt_parts = _split3(dt_all)[:2]
    half = W // 2
    lane_head = lax.broadcasted_iota(jnp.int32, (L, half), 1) // SSM_HEAD_DIM

    for j in range(SSD_GPS):
        wl, wh = j * W, (j + 1) * W
        nl, nh = j * N, (j + 1) * N
        xs = conv_silu(x_ref[:, wl:wh], xh_ref[:, wl:wh], cwx_ref, cbx_ref[:, wl:wh], wl, wh)
        bm = conv_silu(b_ref[:, nl:nh], bh_ref[:, nl:nh], cwb_ref, cbb_ref[:, nl:nh], nl, nh)
        cm = conv_silu(c_ref[:, nl:nh], ch_ref[:, nl:nh], cwc_ref, cbc_ref[:, nl:nh], nl, nh)

        h0 = pl.multiple_of((gp * SSD_GPS + j) * SSM_HPG, SSM_HPG)
        cs_r = cst_ref[pl.ds(h0, SSM_HPG), :]
        cs_g = _dot_parts(cs_parts, sel_ref[j])
        cs_x = _dot_parts(cs_parts, esel_ref[j])
        dt_x = _dot_parts(dt_parts, esel_ref[j])

        xdt = xs * dt_x
        xb = xdt.astype(BF16)
        bb = bm.astype(BF16)
        cb = cm.astype(BF16)
        cbm = lax.dot_general(cb, bb, (((1,), (1,)), ((), ())), preferred_element_type=F32)

        yd = []
        for blk in range(2):
            xblk = xb[:, blk * half:(blk + 1) * half]
            acc = jnp.zeros((L, half), F32)
            for e4 in range(SSM_HPG // 2):
                e = blk * (SSM_HPG // 2) + e4
                diff = cs_g[:, e:e + 1] - cs_r[e:e + 1, :]
                lm = jnp.exp(jnp.where(causal, diff, -jnp.inf))
                m = (cbm * lm).astype(BF16)
                xm = jnp.where(lane_head == e4, xblk, jnp.zeros_like(xblk))
                acc = acc + _dot(m, xm)
            yd.append(acc)
        y = jnp.concatenate(yd, axis=1)

        hprev = h_ref[j]
        y = y + _dot(cb, hprev.astype(BF16)) * jnp.exp(cs_x)
        cs_last = cs_x[L - 1:L, :]
        xdec = (xdt * jnp.exp(cs_last - cs_x)).astype(BF16)
        states = lax.dot_general(bb, xdec, (((0,), (0,)), ((), ())), preferred_element_type=F32)
        h_ref[j] = hprev * jnp.exp(cs_last) + states

        y = y + xs * dsk_ref[:, wl:wh]
        y = y * _silu(z_ref[:, wl:wh].astype(F32))
        ms = jnp.mean(y * y, axis=-1, keepdims=True)
        o_ref[:, wl:wh] = (y * lax.rsqrt(ms + EPS) * ng_ref[:, wl:wh]).astype(o_ref.dtype)


def _ssd_call(zx, dt, alog, conv_w, conv_b, dskip, norm_g):
    S = zx.shape[0]
    P = SSD_GPS
    G, L, W, N = SSM_GROUPS, CHUNK, P * GROUP_W, P * SSM_STATE
    nc = S // L
    xo = D_INNER // W
    bo = 2 * D_INNER // N
    co = bo + G // P
    hb = L // HALO

    def halo(off):
        return lambda g, c: (jnp.maximum(c * hb - 1, 0), off + g)

    cw = conv_w
    cb = conv_b.reshape(1, -1)
    cxo, cbo, cco = 0, D_INNER // N, D_INNER // N + G // P
    in_specs = [
        pl.BlockSpec((L, W), lambda g, c: (c, g)),
        pl.BlockSpec((L, W), lambda g, c: (c, xo + g)),
        pl.BlockSpec((HALO, W), halo(xo)),
        pl.BlockSpec((L, N), lambda g, c: (c, bo + g)),
        pl.BlockSpec((HALO, N), halo(bo)),
        pl.BlockSpec((L, N), lambda g, c: (c, co + g)),
        pl.BlockSpec((HALO, N), halo(co)),
        pl.BlockSpec((L, LANES), lambda g, c: (c, 0)),
        pl.BlockSpec((1, LANES), lambda g, c: (0, 0)),
        pl.BlockSpec((CONV_WIDTH, W), lambda g, c: (0, cxo + g)),
        pl.BlockSpec((1, W), lambda g, c: (0, cxo + g)),
        pl.BlockSpec((CONV_WIDTH, N), lambda g, c: (0, cbo + g)),
        pl.BlockSpec((1, N), lambda g, c: (0, cbo + g)),
        pl.BlockSpec((CONV_WIDTH, N), lambda g, c: (0, cco + g)),
        pl.BlockSpec((1, N), lambda g, c: (0, cco + g)),
        pl.BlockSpec((1, W), lambda g, c: (0, g)),
        pl.BlockSpec((1, W), lambda g, c: (0, g)),
    ]
    return pl.pallas_call(
        _ssd_kernel,
        grid=(G // P, nc),
        in_specs=in_specs,
        out_specs=pl.BlockSpec((L, W), lambda g, c: (c, g)),
        out_shape=jax.ShapeDtypeStruct((S, D_INNER), BF16),
        scratch_shapes=[pltpu.VMEM((P, SSM_STATE, GROUP_W), F32), pltpu.VMEM((LANES, L), F32),
                        pltpu.VMEM((CONV_WIDTH, L, L), BF16), pltpu.VMEM((P, LANES, LANES), BF16),
                        pltpu.VMEM((P, LANES, GROUP_W), BF16)],
        compiler_params=_cparams("parallel", "arbitrary"),
        name="ssd_chunk_scan",
    )(zx, zx, zx, zx, zx, zx, zx, dt, alog, cw, cb, cw, cb, cw, cb, dskip, norm_g)


def _attn_kernel(q_ref, k_ref, v_ref, lq1_ref, lk1_ref, lq2_ref, lk2_ref, sg_ref,
                 o_ref, m_ref, l_ref, acc_ref, p_ref, alpha_ref, *, lambda_init):
    qi = pl.program_id(1)
    T = ATT_T
    HD = DA_HEAD_DIM
    m_ref[...] = jnp.full_like(m_ref, -jnp.inf)
    l_ref[...] = jnp.zeros_like(l_ref)
    acc_ref[...] = jnp.zeros_like(acc_ref)

    def softmax_slab(kj, idx, masked):
        r0 = pl.multiple_of(kj * T, T)
        t = idx % 2
        q = q_ref[:, idx * HD:(idx + 1) * HD]
        kt = k_ref[pl.ds(r0, T), t * HD:(t + 1) * HD]
        s = lax.dot_general(q, kt, (((1,), (1,)), ((), ())), preferred_element_type=F32)
        if masked:
            row = lax.broadcasted_iota(jnp.int32, (T, T), 0)
            col = lax.broadcasted_iota(jnp.int32, (T, T), 1)
            s = jnp.where(col <= row, s, -jnp.inf)
        m_prev = m_ref[idx]
        m_new = jnp.maximum(m_prev, jnp.max(s, axis=-1, keepdims=True))
        alpha_ref[idx] = jnp.exp2(m_prev - m_new)
        p = jnp.exp2(s - _lane_tile(m_new, T // LANES))
        psum = p[:, 0:LANES]
        for b in range(1, T // LANES):
            psum = psum + p[:, b * LANES:(b + 1) * LANES]
        l_ref[idx] = alpha_ref[idx] * l_ref[idx] + psum
        p_ref[idx] = p.astype(BF16)
        m_ref[idx] = m_new

    def pv_slab(kj, idx):
        r0 = pl.multiple_of(kj * T, T)
        v = v_ref[pl.ds(r0, T), :]
        acc_ref[idx] = _lane_tile(alpha_ref[idx], DA_V_DIM // LANES) * acc_ref[idx] + _dot(p_ref[idx], v)

    n_slab = 2 * DA_GROUP
    for idx in range(n_slab):
        softmax_slab(qi, idx, True)

    def trip(prev, kj):
        for idx in range(n_slab):
            pv_slab(prev, idx)
            softmax_slab(kj, idx, False)

    rem = qi % 2

    @pl.when(rem == 1)
    def _():
        trip(qi, 0)

    def body(i, prev):
        k0 = rem + 2 * i
        trip(prev, k0)
        trip(k0, k0 + 1)
        return k0 + 1

    last = lax.fori_loop(0, qi // 2, body, jnp.where(rem == 1, 0, qi))
    for idx in range(n_slab):
        pv_slab(last, idx)

    lam = (jnp.exp(jnp.sum(lq1_ref[...] * lk1_ref[...], axis=-1, keepdims=True))
           - jnp.exp(jnp.sum(lq2_ref[...] * lk2_ref[...], axis=-1, keepdims=True)) + lambda_init)
    for gq in range(DA_GROUP):
        a0 = acc_ref[gq * 2] / jnp.sum(l_ref[gq * 2], axis=-1, keepdims=True)
        a1 = acc_ref[gq * 2 + 1] / jnp.sum(l_ref[gq * 2 + 1], axis=-1, keepdims=True)
        o = a0 - lam * a1
        ms = jnp.mean(o * o, axis=-1, keepdims=True)
        o = (o * lax.rsqrt(ms + EPS) * sg_ref[...]) * (1.0 - lambda_init)
        o_ref[:, gq * DA_V_DIM:(gq + 1) * DA_V_DIM] = o.astype(o_ref.dtype)


def _attn_call(q, kv, lq1, lk1, lq2, lk2, subln_g, lambda_init):
    S = q.shape[0]
    v_off = K_DIM // DA_V_DIM
    T = ATT_T
    qw = DA_GROUP * 2 * DA_HEAD_DIM
    kw = 2 * DA_HEAD_DIM
    vec = pl.BlockSpec((1, DA_HEAD_DIM), lambda h, i: (0, 0))
    return pl.pallas_call(
        functools.partial(_attn_kernel, lambda_init=lambda_init),
        grid=(DA_KV_HEADS, S // T),
        in_specs=[pl.BlockSpec((T, qw), lambda h, i: (i, h)),
                  pl.BlockSpec((S, kw), lambda h, i: (0, h)),
                  pl.BlockSpec((S, DA_V_DIM), lambda h, i: (0, v_off + h)),
                  vec, vec, vec, vec,
                  pl.BlockSpec((1, DA_V_DIM), lambda h, i: (0, 0))],
        out_specs=pl.BlockSpec((T, DA_GROUP * DA_V_DIM), lambda h, i: (i, h)),
        out_shape=jax.ShapeDtypeStruct((S, DA_HEADS * DA_V_DIM), BF16),
        scratch_shapes=[pltpu.VMEM((2 * DA_GROUP, T, LANES), F32),
                        pltpu.VMEM((2 * DA_GROUP, T, LANES), F32),
                        pltpu.VMEM((2 * DA_GROUP, T, DA_V_DIM), F32),
                        pltpu.VMEM((2 * DA_GROUP, T, T), BF16),
                        pltpu.VMEM((2 * DA_GROUP, T, LANES), F32)],
        compiler_params=_cparams("parallel", "parallel"),
        name="diff_attention",
    )(q, kv, kv, lq1, lk1, lq2, lk2, subln_g)


def _router_kernel(x_ref, g_ref, sh_ref, sc_ref, rw_ref, rb_ref, h_ref, idx_ref, wt_ref, cnt_ref, carry_ref,
                   *, n_real):
    i = pl.program_id(0)

    @pl.when(i == 0)
    def _():
        carry_ref[...] = jnp.zeros_like(carry_ref)

    @pl.when(i >= n_real)
    def _():
        h_ref[...] = jnp.zeros_like(h_ref)

    @pl.when(i < n_real)
    def _():
        _route(x_ref, g_ref, sh_ref, sc_ref, rw_ref, rb_ref, h_ref, idx_ref, wt_ref, cnt_ref, carry_ref)


def _route(x_ref, g_ref, sh_ref, sc_ref, rw_ref, rb_ref, h_ref, idx_ref, wt_ref, cnt_ref, carry_ref):
    h = _norm_mod(x_ref[...], g_ref[...], sh_ref[...], sc_ref[...])
    h_ref[...] = h.astype(BF16)
    hh = h.astype(BF16)
    hl = (h - hh.astype(F32)).astype(BF16)
    rw = rw_ref[...]
    wh = rw.astype(BF16)
    wl = (rw - wh.astype(F32)).astype(BF16)
    logits = _dot(hh, wh) + _dot(hl, wh) + _dot(hh, wl)
    lt = logits.T[0:N_EXPERTS, :]
    score = _sigmoid(lt)
    sel = score + rb_ref[...]
    srow = [score[e:e + 1, :] for e in range(N_EXPERTS)]
    row = [sel[e:e + 1, :] for e in range(N_EXPERTS)]

    gscore = []
    for gi in range(N_EXPERT_GROUPS):
        r = row[gi * EXPERTS_PER_GROUP:(gi + 1) * EXPERTS_PER_GROUP]
        best = None
        for a in range(EXPERTS_PER_GROUP):
            for b in range(a + 1, EXPERTS_PER_GROUP):
                s = r[a] + r[b]
                best = s if best is None else jnp.maximum(best, s)
        gscore.append(best)
    bestg = jnp.zeros_like(gscore[0], dtype=jnp.int32)
    bestv = gscore[0]
    for gi in range(1, N_EXPERT_GROUPS):
        upd = gscore[gi] > bestv
        bestg = jnp.where(upd, gi, bestg)
        bestv = jnp.where(upd, gscore[gi], bestv)

    cand = [jnp.where(bestg == e // EXPERTS_PER_GROUP, row[e], -jnp.inf) for e in range(N_EXPERTS)]

    def first_argmax(vals):
        bi = jnp.zeros_like(bestg)
        bv = vals[0]
        for e in range(1, N_EXPERTS):
            upd = vals[e] > bv
            bi = jnp.where(upd, e, bi)
            bv = jnp.where(upd, vals[e], bv)
        return bi

    i1 = first_argmax(cand)
    i2 = first_argmax([jnp.where(i1 == e, -jnp.inf, cand[e]) for e in range(N_EXPERTS)])
    s1 = sum(jnp.where(i1 == e, srow[e], 0.0) for e in range(N_EXPERTS))
    s2 = sum(jnp.where(i2 == e, srow[e], 0.0) for e in range(N_EXPERTS))
    den = s1 + s2
    tm = i1.shape[1]
    e_id = lax.broadcasted_iota(jnp.int32, (LANES, tm), 0)
    oh1 = jnp.where(e_id == i1, 1.0, 0.0)
    oh2 = jnp.where(e_id == i2, 1.0, 0.0)
    oh = (oh1 + oh2).T
    r_t = lax.broadcasted_iota(jnp.int32, (tm, tm), 0)
    c_t = lax.broadcasted_iota(jnp.int32, (tm, tm), 1)
    tril = jnp.where(r_t >= c_t, 1.0, 0.0).astype(BF16)
    incl = _dot(tril, oh.astype(BF16))
    carry = carry_ref[0:1, :]
    base = (incl - oh + carry).T
    rank1 = jnp.sum(oh1 * base, axis=0, keepdims=True).astype(jnp.int32)
    rank2 = jnp.sum(oh2 * base, axis=0, keepdims=True).astype(jnp.int32)
    total = carry + incl[tm - 1:tm, :]
    carry_ref[...] = jnp.broadcast_to(total, carry_ref.shape)
    cnt_ref[...] = jnp.broadcast_to(total, cnt_ref.shape).astype(jnp.int32)
    idx_ref[...] = jnp.concatenate([i1, i2, rank1, rank2, jnp.zeros((4, tm), jnp.int32)], axis=0)
    wt_ref[...] = jnp.concatenate([s1 / den, s2 / den, jnp.zeros((LANES - 2, tm), F32)], axis=0).T


def _router_call(x, g, sh, sc, rw_pad, rb_col, h_rows):
    S, D = x.shape
    tm = ROUTER_TM
    n_real = S // tm
    last = n_real - 1
    vec = pl.BlockSpec((1, D), lambda i: (0, 0))
    return pl.pallas_call(
        functools.partial(_router_kernel, n_real=n_real),
        grid=(h_rows // tm,),
        in_specs=[pl.BlockSpec((tm, D), lambda i: (jnp.minimum(i, last), 0)), vec, vec, vec,
                  pl.BlockSpec((D, LANES), lambda i: (0, 0)),
                  pl.BlockSpec((N_EXPERTS, 1), lambda i: (0, 0))],
        out_specs=[pl.BlockSpec((tm, D), lambda i: (i, 0)),
                   pl.BlockSpec((8, tm), lambda i: (0, jnp.minimum(i, last))),
                   pl.BlockSpec((tm, LANES), lambda i: (jnp.minimum(i, last), 0)),
                   pl.BlockSpec((8, LANES), lambda i: (0, 0))],
        out_shape=[jax.ShapeDtypeStruct((h_rows, D), BF16),
                   jax.ShapeDtypeStruct((8, S), jnp.int32),
                   jax.ShapeDtypeStruct((S, LANES), F32),
                   jax.ShapeDtypeStruct((8, LANES), jnp.int32)],
        scratch_shapes=[pltpu.VMEM((8, LANES), F32)],
        compiler_params=_cparams("arbitrary"),
        name="moe_norm_router",
    )(x, g, sh, sc, rw_pad, rb_col)


def _moe_kernel(ti_ref, te_ref, tf_ref, nv_ref, x_ref, wg_ref, wu_ref, wd_ref, o_ref,
                wgb_ref, wub_ref, wdb_ref):
    t = pl.program_id(0)

    @pl.when(t < nv_ref[0])
    def _():
        @pl.when(tf_ref[t] == 1)
        def _():
            wgb_ref[...] = wg_ref[...].astype(BF16)
            wub_ref[...] = wu_ref[...].astype(BF16)
            wdb_ref[...] = wd_ref[...].astype(BF16)

        x = x_ref[...]
        gt = _dot(x, wgb_ref[...])
        up = _dot(x, wub_ref[...])
        h = _silu(gt) * up
        o_ref[...] = _dot(h.astype(BF16), wdb_ref[...]).astype(o_ref.dtype)

    @pl.when(t >= nv_ref[0])
    def _():
        o_ref[...] = jnp.zeros_like(o_ref)


def _moe_call(xs, tile_idx, tile_expert, tile_first, n_valid, w_gate, w_up, w_down, layer):
    R, D = xs.shape
    tm = MOE_TM
    F = D_EXPERT
    grid_spec = pltpu.PrefetchScalarGridSpec(
        num_scalar_prefetch=4,
        grid=(R // tm,),
        in_specs=[pl.BlockSpec((tm, D), lambda t, ti, te, tf, nv: (ti[t], 0)),
                  pl.BlockSpec((None, None, D, F), lambda t, ti, te, tf, nv: (layer, te[t], 0, 0)),
                  pl.BlockSpec((None, None, D, F), lambda t, ti, te, tf, nv: (layer, te[t], 0, 0)),
                  pl.BlockSpec((None, None, F, D), lambda t, ti, te, tf, nv: (layer, te[t], 0, 0))],
        out_specs=pl.BlockSpec((tm, D), lambda t, ti, te, tf, nv: (t, 0)),
        scratch_shapes=[pltpu.VMEM((D, F), BF16), pltpu.VMEM((D, F), BF16), pltpu.VMEM((F, D), BF16)],
    )
    return pl.pallas_call(
        _moe_kernel,
        grid_spec=grid_spec,
        out_shape=jax.ShapeDtypeStruct((R, D), BF16),
        compiler_params=_cparams("arbitrary"),
        name="moe_experts",
    )(tile_idx, tile_expert, tile_first, n_valid, xs, w_gate, w_up, w_down)


def _combine_kernel(x_ref, y1_ref, y2_ref, wt_ref, g_ref, fg_ref, o_ref, *, final):
    wt = wt_ref[...]
    y = wt[:, 0:1] * y1_ref[...].astype(F32) + wt[:, 1:2] * y2_ref[...].astype(F32)
    x = x_ref[...] + g_ref[...] * y
    if final:
        ms = jnp.mean(x * x, axis=-1, keepdims=True)
        x = x * lax.rsqrt(ms + EPS) * fg_ref[...]
    o_ref[...] = x


def _combine_call(x, y1, y2, wts, gate, final_g, final, part):
    S, D = x.shape
    tm = COMBINE_TM
    off = part * (y1.shape[0] // tm)
    blk_x = pl.BlockSpec((tm, D), lambda i: (i + off, 0))
    blk_y = pl.BlockSpec((tm, D), lambda i: (i, 0))
    vec = pl.BlockSpec((1, D), lambda i: (0, 0))
    return pl.pallas_call(
        functools.partial(_combine_kernel, final=final),
        grid=(y1.shape[0] // tm,),
        in_specs=[blk_x, blk_y, blk_y, pl.BlockSpec((tm, LANES), lambda i: (i + off, 0)), vec, vec],
        out_specs=blk_x,
        out_shape=jax.ShapeDtypeStruct((S, D), F32),
        input_output_aliases={0: 0},
        compiler_params=_cparams("parallel"),
        name="moe_combine",
    )(x, y1, y2, wts, gate, final_g)


def _take_rows(a, rows):
    return a.at[rows].get(mode="promise_in_bounds")


def _moe_rows(S):
    return TOP_K * S + N_EXPERTS * MOE_TM


def _dispatch_plan(idx, counts, S):
    tm = MOE_TM
    R = _moe_rows(S)
    n_tiles = R // tm
    e_flat = idx[:TOP_K].reshape(-1)
    rank = idx[TOP_K:2 * TOP_K].reshape(-1)
    tok_flat = jnp.tile(jnp.arange(S, dtype=jnp.int32), TOP_K)
    onehot = (e_flat[:, None] == jnp.arange(N_EXPERTS, dtype=jnp.int32)[None, :]).astype(jnp.int32)
    tiles_e = (counts + tm - 1) // tm
    tile_end = jnp.cumsum(tiles_e)
    tile_start = tile_end - tiles_e
    dest = jnp.sum(onehot * (tile_start * tm)[None, :], axis=1) + rank
    src_tok = (jnp.arange(R, dtype=jnp.int32) % S).at[dest].set(
        tok_flat, mode="promise_in_bounds", unique_indices=True)
    n_valid = tile_end[-1]
    t_ids = jnp.minimum(jnp.arange(n_tiles, dtype=jnp.int32), n_valid - 1)
    tile_expert = jnp.sum((t_ids[:, None] >= tile_end[None, :]).astype(jnp.int32), axis=1)
    tile_first = jnp.sum(((t_ids[:, None] == tile_start[None, :]) & (tiles_e[None, :] > 0)).astype(jnp.int32),
                         axis=1)
    return (src_tok, t_ids, tile_expert, tile_first, n_valid.reshape(1).astype(jnp.int32), dest[:S], dest[S:])


def _moe_block(x, g, sh, sc, gate, rw_pad, rb_col, w_gate, w_up, w_down, layer, final_g, final):
    S = x.shape[0]
    h, idx, wts, cnt = _router_call(x, g, sh, sc, rw_pad, rb_col, _moe_rows(S) + ROUTER_TM)
    src_tok, t_ids, t_exp, t_first, n_valid, pos1, pos2 = _dispatch_plan(idx, cnt[0, :N_EXPERTS], S)
    xs = _take_rows(h, src_tok)
    ys = _moe_call(xs, t_ids, t_exp, t_first, n_valid, w_gate, w_up, w_down, layer)
    n = S // COMBINE_PARTS
    for part in range(COMBINE_PARTS):
        y1 = _take_rows(ys, pos1[part * n:(part + 1) * n])
        y2 = _take_rows(ys, pos2[part * n:(part + 1) * n])
        x = _combine_call(x, y1, y2, wts, gate, final_g, final, part)
    return x


def _rope_tables(S):
    inv = 1.0 / (ROPE_THETA ** (jnp.arange(0, ROPE_DIM, 2, dtype=F32) / ROPE_DIM))
    ang = jnp.arange(S, dtype=F32)[:, None] * inv[None, :]
    cos, sin = jnp.cos(ang), jnp.sin(ang)
    zeros = jnp.zeros((S, LANES - ROPE_DIM), F32)
    zh = jnp.zeros((S, ROPE_HALF), F32)
    cos_t = jnp.concatenate([cos, cos, jnp.ones((S, LANES - ROPE_DIM), F32)], axis=1)
    sin_a = jnp.concatenate([-sin, zh, zeros], axis=1)
    sin_b = jnp.concatenate([zh, sin, zeros], axis=1)
    return cos_t, sin_a, sin_b


def kernel(x, c, mod_w, mod_b, norm1_g, norm2_g, ssm_in_w, ssm_conv_w, ssm_conv_b, ssm_dt_bias, ssm_a_log, ssm_d, ssm_norm_g, ssm_out_w, kv_mod_w, kv_mod_b, kv_norm_g, w_kv, attn_q_w, lam_q1, lam_k1, lam_q2, lam_k2, subln_g, attn_o_w, moe_w_gate, moe_w_up, moe_w_down, router_w, router_b, final_g):
    _, S, D = x.shape
    xr = x[0]
    cb = jnp.broadcast_to(c[0][:, None], (D, LANES))
    mods = _mod_call(cb, mod_w, mod_b)
    kvm = _mod_call(cb, kv_mod_w[None], kv_mod_b[None])
    cos_t, sin_a, sin_b = _rope_tables(S)
    rw_pad = jnp.pad(router_w, ((0, 0), (0, LANES - N_EXPERTS)))
    rb_col = router_b.reshape(N_EXPERTS, 1)
    row = lambda v: v.reshape(1, -1)
    pad_l = lambda v: jnp.pad(v.reshape(1, -1), ((0, 0), (0, LANES - v.shape[-1])))

    kv = None
    for layer in range(DEPTH):
        m = mods[layer]
        sh1, sc1, g1, sh2, sc2, g2 = [m[:, i * D:(i + 1) * D] for i in range(N_MOD)]
        if layer == N_A_LAYERS:
            kv_sh, kv_sc = kvm[0][:, :D], kvm[0][:, D:]
            kv = _rope_proj_call(xr, row(kv_norm_g), kv_sh, kv_sc, w_kv.astype(BF16), cos_t, sin_a, sin_b,
                                 rope_cols=K_DIM, scale=1.0, name="kv_proj")
        if layer < N_A_LAYERS:
            i = layer
            w_in = ssm_in_w[i]
            zx, dt = _in_proj_call(xr, row(norm1_g[layer]), sh1, sc1,
                                   w_in.astype(BF16),
                                   jnp.pad(w_in[:, ZX_DIM:], ((0, 0), (0, LANES - SSM_HEADS))).astype(BF16),
                                   pad_l(ssm_dt_bias[i]))
            y = _ssd_call(zx, dt, pad_l(ssm_a_log[i]), ssm_conv_w[i], ssm_conv_b[i],
                          row(jnp.repeat(ssm_d[i], SSM_HEAD_DIM)), row(ssm_norm_g[i]))
            xr = _res_mm_call(y, ssm_out_w[i].astype(BF16), xr, g1, name="ssm_out_proj")
        else:
            j = layer - N_A_LAYERS
            lambda_init = 0.8 - 0.6 * float(np.exp(-0.3 * layer))
            q = _rope_proj_call(xr, row(norm1_g[layer]), sh1, sc1, attn_q_w[j].astype(BF16),
                                cos_t, sin_a, sin_b, rope_cols=D, scale=DA_HEAD_DIM ** -0.5 * LOG2E,
                                name="q_proj")
            o = _attn_call(q, kv, row(lam_q1[j]), row(lam_k1[j]), row(lam_q2[j]), row(lam_k2[j]),
                           row(subln_g[j]), lambda_init)
            xr = _res_mm_call(o, attn_o_w[j].astype(BF16), xr, g1, name="attn_o_proj")
        xr = _moe_block(xr, row(norm2_g[layer]), sh2, sc2, g2, rw_pad, rb_col,
                        moe_w_gate, moe_w_up, moe_w_down, layer, row(final_g), layer == DEPTH - 1)
    return xr[None]
```

```python
import functools

import numpy as np
import jax
import jax.numpy as jnp
from jax import lax
from jax.experimental import pallas as pl
from jax.experimental.pallas import tpu as pltpu

F32 = jnp.float32
BF16 = jnp.bfloat16

D_MODEL = 2048
DEPTH = 4
N_A_LAYERS = DEPTH // 2
EPS = 1e-5
N_MOD = 6

D_INNER = 2 * D_MODEL
SSM_HEAD_DIM = 64
SSM_HEADS = D_INNER // SSM_HEAD_DIM
SSM_GROUPS = 8
SSM_HPG = SSM_HEADS // SSM_GROUPS
SSM_STATE = 128
CONV_WIDTH = 4
CHUNK = 256
GROUP_W = SSM_HPG * SSM_HEAD_DIM
ZX_DIM = 2 * D_INNER + 2 * SSM_GROUPS * SSM_STATE

DA_HEADS = 8
DA_KV_HEADS = 4
DA_GROUP = DA_HEADS // DA_KV_HEADS
DA_HEAD_DIM = D_MODEL // DA_HEADS // 2
DA_V_DIM = 2 * DA_HEAD_DIM
K_DIM = DA_KV_HEADS * 2 * DA_HEAD_DIM
V_DIM = DA_KV_HEADS * DA_V_DIM
ROPE_DIM = DA_HEAD_DIM // 4
ROPE_HALF = ROPE_DIM // 2
ROPE_THETA = 500000.0

N_EXPERTS = 16
N_EXPERT_GROUPS = 4
EXPERTS_PER_GROUP = N_EXPERTS // N_EXPERT_GROUPS
TOP_K = 2
D_EXPERT = 512

LOG2E = 1.4426950408889634
LANES = 128
HALO = 16
VMEM_LIMIT = 48 * 1024 * 1024

PROJ_TM = 1024
PROJ_TN = 512
WIDE_TN = 1024
ATT_T = 512
ATT_UNROLL = 4
MOE_TM = 256
SSD_GPS = 8
ROUTER_TM = 512
COMBINE_TM = 512
COMBINE_PARTS = 2


def _cparams(*sem):
    return pltpu.CompilerParams(dimension_semantics=sem, vmem_limit_bytes=VMEM_LIMIT)


def _sigmoid(v):
    return 1.0 / (1.0 + jnp.exp(-v))


def _silu(v):
    hv = 0.5 * v
    return hv * (1.0 + jnp.tanh(hv))


def _lane_tile(v, n):
    return v if n == 1 else jnp.concatenate([v] * n, axis=1)


def _norm_mod(x, g, sh, sc):
    ms = jnp.mean(x * x, axis=-1, keepdims=True)
    return (x * lax.rsqrt(ms + EPS) * g) * (1.0 + sc) + sh


def _split3(v):
    hi = v.astype(BF16)
    r1 = v - hi.astype(F32)
    mid = r1.astype(BF16)
    lo = (r1 - mid.astype(F32)).astype(BF16)
    return hi, mid, lo


def _dot(a, b):
    return jnp.dot(a, b, preferred_element_type=F32)


def _dot_parts(parts, sel):
    acc = _dot(parts[0], sel)
    for p in parts[1:]:
        acc = acc + _dot(p, sel)
    return acc


def _dot3_l(sel, v):
    hi, mid, lo = _split3(v)
    return _dot(sel, hi) + _dot(sel, mid) + _dot(sel, lo)


def _mod_kernel(c_ref, w_ref, b_ref, o_ref):
    cc = c_ref[...]
    sc = _silu(cc)
    w = w_ref[...]
    o_ref[...] = jnp.sum(w * _lane_tile(sc, w.shape[1] // LANES), axis=0, keepdims=True) + b_ref[...]


def _mod_call(cb, w, b):
    L, D, N = w.shape
    tn = 1024
    return pl.pallas_call(
        _mod_kernel,
        grid=(L, N // tn),
        in_specs=[pl.BlockSpec((D, LANES), lambda l, j: (0, 0)),
                  pl.BlockSpec((None, D, tn), lambda l, j: (l, 0, j)),
                  pl.BlockSpec((None, 1, tn), lambda l, j: (l, 0, j))],
        out_specs=pl.BlockSpec((None, 1, tn), lambda l, j: (l, 0, j)),
        out_shape=jax.ShapeDtypeStruct((L, 1, N), F32),
        compiler_params=_cparams("parallel", "parallel"),
        name="mod_vectors",
    )(cb, w, b.reshape(L, 1, N))


def _norm_prologue(x_ref, g_ref, sh_ref, sc_ref, h_ref):
    tm = x_ref.shape[0]
    rc = 256
    for r in range(0, tm, rc):
        h = _norm_mod(x_ref[r:r + rc, :], g_ref[...], sh_ref[...], sc_ref[...])
        h_ref[r:r + rc, :] = h.astype(BF16)


def _in_proj_kernel(x_ref, g_ref, sh_ref, sc_ref, w_ref, wdt_ref, dtb_ref, o_ref, dt_ref, h_ref):
    @pl.when(pl.program_id(1) == 0)
    def _():
        _norm_prologue(x_ref, g_ref, sh_ref, sc_ref, h_ref)
        d = _dot(h_ref[...], wdt_ref[...]) + dtb_ref[...]
        dt_ref[...] = jnp.maximum(d, 0.0) + jnp.log1p(jnp.exp(-jnp.abs(d)))

    o_ref[...] = _dot(h_ref[...], w_ref[...]).astype(o_ref.dtype)


def _in_proj_call(x, g, sh, sc, w, wdt, dtb):
    M, K = x.shape
    N = ZX_DIM
    tm, tn = PROJ_TM, WIDE_TN
    vec = pl.BlockSpec((1, K), lambda i, j: (0, 0))
    return pl.pallas_call(
        _in_proj_kernel,
        grid=(M // tm, N // tn),
        in_specs=[pl.BlockSpec((tm, K), lambda i, j: (i, 0)), vec, vec, vec,
                  pl.BlockSpec((K, tn), lambda i, j: (0, j)),
                  pl.BlockSpec((K, LANES), lambda i, j: (0, 0)),
                  pl.BlockSpec((1, LANES), lambda i, j: (0, 0))],
        out_specs=[pl.BlockSpec((tm, tn), lambda i, j: (i, j)),
                   pl.BlockSpec((tm, LANES), lambda i, j: (i, 0))],
        out_shape=[jax.ShapeDtypeStruct((M, N), BF16), jax.ShapeDtypeStruct((M, LANES), F32)],
        scratch_shapes=[pltpu.VMEM((tm, K), BF16)],
        compiler_params=_cparams("parallel", "arbitrary"),
        name="ssm_in_proj",
    )(x, g, sh, sc, w, wdt, dtb)


def _rope_proj_kernel(x_ref, g_ref, sh_ref, sc_ref, w_ref, cos_ref, sa_ref, sb_ref, o_ref, h_ref,
                      *, rope_tiles, n_tiles, scale):
    j = pl.program_id(1)

    @pl.when(j == 0)
    def _():
        _norm_prologue(x_ref, g_ref, sh_ref, sc_ref, h_ref)

    acc = _dot(h_ref[...], w_ref[...])
    tn = acc.shape[1]
    nb = tn // LANES

    def roped():
        cos = _lane_tile(cos_ref[...], nb)
        sa = _lane_tile(sa_ref[...], nb)
        sb = _lane_tile(sb_ref[...], nb)
        r = acc * cos + pltpu.roll(acc, tn - ROPE_HALF, 1) * sa + pltpu.roll(acc, ROPE_HALF, 1) * sb
        return (r * scale).astype(o_ref.dtype)

    if rope_tiles >= n_tiles:
        o_ref[...] = roped()
    else:
        @pl.when(j < rope_tiles)
        def _():
            o_ref[...] = roped()

        @pl.when(j >= rope_tiles)
        def _():
            o_ref[...] = acc.astype(o_ref.dtype)


def _rope_proj_call(x, g, sh, sc, w, cos, sa, sb, *, rope_cols, scale, name):
    M, K = x.shape
    N = w.shape[1]
    tm, tn = PROJ_TM, WIDE_TN
    vec = pl.BlockSpec((1, K), lambda i, j: (0, 0))
    tab = pl.BlockSpec((tm, LANES), lambda i, j: (i, 0))
    kern = functools.partial(_rope_proj_kernel, rope_tiles=rope_cols // tn, n_tiles=N // tn, scale=scale)
    return pl.pallas_call(
        kern,
        grid=(M // tm, N // tn),
        in_specs=[pl.BlockSpec((tm, K), lambda i, j: (i, 0)), vec, vec, vec,
                  pl.BlockSpec((K, tn), lambda i, j: (0, j)), tab, tab, tab],
        out_specs=pl.BlockSpec((tm, tn), lambda i, j: (i, j)),
        out_shape=jax.ShapeDtypeStruct((M, N), BF16),
        scratch_shapes=[pltpu.VMEM((tm, K), BF16)],
        compiler_params=_cparams("parallel", "arbitrary"),
        name=name,
    )(x, g, sh, sc, w, cos, sa, sb)


def _res_mm_kernel(a_ref, w_ref, res_ref, gate_ref, o_ref):
    o_ref[...] = res_ref[...] + gate_ref[...] * _dot(a_ref[...], w_ref[...])


def _res_mm_call(a, w, res, gate, *, name):
    M, K = a.shape
    N = w.shape[1]
    tm, tn = PROJ_TM, (WIDE_TN if K <= D_MODEL else PROJ_TN)
    return pl.pallas_call(
        _res_mm_kernel,
        grid=(M // tm, N // tn),
        in_specs=[pl.BlockSpec((tm, K), lambda i, j: (i, 0)),
                  pl.BlockSpec((K, tn), lambda i, j: (0, j)),
                  pl.BlockSpec((tm, tn), lambda i, j: (i, j)),
                  pl.BlockSpec((1, tn), lambda i, j: (0, j))],
        out_specs=pl.BlockSpec((tm, tn), lambda i, j: (i, j)),
        out_shape=jax.ShapeDtypeStruct((M, N), F32),
        compiler_params=_cparams("parallel", "parallel"),
        name=name,
    )(a, w, res, gate)


def _ssd_kernel(z_ref, x_ref, xh_ref, b_ref, bh_ref, c_ref, ch_ref, dt_ref, alog_ref,
                cwx_ref, cbx_ref, cwb_ref, cbb_ref, cwc_ref, cbc_ref, dsk_ref, ng_ref,
                o_ref, h_ref, cst_ref, shift_ref, sel_ref, esel_ref):
    gp = pl.program_id(0)
    c = pl.program_id(1)
    L, W, N = CHUNK, GROUP_W, SSM_STATE
    row = lax.broadcasted_iota(jnp.int32, (L, L), 0)
    col = lax.broadcasted_iota(jnp.int32, (L, L), 1)
    causal = row >= col

    @pl.when(c == 0)
    def _():
        h_ref[...] = jnp.zeros_like(h_ref)
        shift_ref[0] = jnp.where(causal, 1.0, 0.0).astype(BF16)
        for k in range(1, CONV_WIDTH):
            shift_ref[k] = jnp.where(row - col == k, 1.0, 0.0).astype(BF16)
        k1 = lax.broadcasted_iota(jnp.int32, (LANES, LANES), 0)
        n1 = lax.broadcasted_iota(jnp.int32, (LANES, LANES), 1)
        k2 = lax.broadcasted_iota(jnp.int32, (LANES, W), 0)
        n2 = lax.broadcasted_iota(jnp.int32, (LANES, W), 1) // SSM_HEAD_DIM
        for j in range(SSD_GPS):
            h0 = (gp * SSD_GPS + j) * SSM_HPG
            sel_ref[j] = jnp.where(jnp.where(n1 < SSM_HPG, k1 - n1, -1) == h0, 1.0, 0.0).astype(BF16)
            esel_ref[j] = jnp.where(k2 - n2 == h0, 1.0, 0.0).astype(BF16)

    def conv_silu(cur_b, halo_b, w_ref, bias, lo, hi):
        acc = bias + w_ref[CONV_WIDTH - 1:CONV_WIDTH, lo:hi] * cur_b.astype(F32)
        prev8 = jnp.where(c == 0, 0.0, halo_b.astype(F32))[HALO - 8:HALO, :]
        row8 = lax.broadcasted_iota(jnp.int32, prev8.shape, 0)
        top = jnp.zeros_like(prev8)
        for k in range(1, CONV_WIDTH):
            wk = w_ref[CONV_WIDTH - 1 - k:CONV_WIDTH - k, lo:hi]
            acc = acc + wk * _dot(shift_ref[k], cur_b)
            top = top + wk * jnp.where(row8 < k, pltpu.roll(prev8, k, 0), 0.0)
        acc = jnp.concatenate([acc[0:8, :] + top, acc[8:, :]], axis=0)
        return _silu(acc)

    dt_all = dt_ref[...]
    dta = dt_all * (-jnp.exp(alog_ref[...]))
    cs_all = _dot3_l(shift_ref[0], dta)
    cst_ref[...] = cs_all.T
    cs_parts = _split3(cs_all)
    dt_parts = _split3(dt_all)[:2]
    half = W // 2
    lane_head = lax.broadcasted_iota(jnp.int32, (L, half), 1) // SSM_HEAD_DIM

    for j in range(SSD_GPS):
        wl, wh = j * W, (j + 1) * W
        nl, nh = j * N, (j + 1) * N
        xs = conv_silu(x_ref[:, wl:wh], xh_ref[:, wl:wh], cwx_ref, cbx_ref[:, wl:wh], wl, wh)
        bm = conv_silu(b_ref[:, nl:nh], bh_ref[:, nl:nh], cwb_ref, cbb_ref[:, nl:nh], nl, nh)
        cm = conv_silu(c_ref[:, nl:nh], ch_ref[:, nl:nh], cwc_ref, cbc_ref[:, nl:nh], nl, nh)

        h0 = pl.multiple_of((gp * SSD_GPS + j) * SSM_HPG, SSM_HPG)
        cs_r = cst_ref[pl.ds(h0, SSM_HPG), :]
        cs_g = _dot_parts(cs_parts, sel_ref[j])
        cs_x = _dot_parts(cs_parts, esel_ref[j])
        dt_x = _dot_parts(dt_parts, esel_ref[j])

        xdt = xs * dt_x
        xb = xdt.astype(BF16)
        bb = bm.astype(BF16)
        cb = cm.astype(BF16)
        cbm = lax.dot_general(cb, bb, (((1,), (1,)), ((), ())), preferred_element_type=F32)

        yd = []
        for blk in range(2):
            xblk = xb[:, blk * half:(blk + 1) * half]
            acc = jnp.zeros((L, half), F32)
            for e4 in range(SSM_HPG // 2):
                e = blk * (SSM_HPG // 2) + e4
                diff = cs_g[:, e:e + 1] - cs_r[e:e + 1, :]
                lm = jnp.exp(jnp.where(causal, diff, -jnp.inf))
                m = (cbm * lm).astype(BF16)
                xm = jnp.where(lane_head == e4, xblk, jnp.zeros_like(xblk))
                acc = acc + _dot(m, xm)
            yd.append(acc)
        y = jnp.concatenate(yd, axis=1)

        hprev = h_ref[j]
        y = y + _dot(cb, hprev.astype(BF16)) * jnp.exp(cs_x)
        cs_last = cs_x[L - 1:L, :]
        xdec = (xdt * jnp.exp(cs_last - cs_x)).astype(BF16)
        states = lax.dot_general(bb, xdec, (((0,), (0,)), ((), ())), preferred_element_type=F32)
        h_ref[j] = hprev * jnp.exp(cs_last) + states

        y = y + xs * dsk_ref[:, wl:wh]
        y = y * _silu(z_ref[:, wl:wh].astype(F32))
        ms = jnp.mean(y * y, axis=-1, keepdims=True)
        o_ref[:, wl:wh] = (y * lax.rsqrt(ms + EPS) * ng_ref[:, wl:wh]).astype(o_ref.dtype)


def _ssd_call(zx, dt, alog, conv_w, conv_b, dskip, norm_g):
    S = zx.shape[0]
    P = SSD_GPS
    G, L, W, N = SSM_GROUPS, CHUNK, P * GROUP_W, P * SSM_STATE
    nc = S // L
    xo = D_INNER // W
    bo = 2 * D_INNER // N
    co = bo + G // P
    hb = L // HALO

    def halo(off):
        return lambda g, c: (jnp.maximum(c * hb - 1, 0), off + g)

    cw = conv_w
    cb = conv_b.reshape(1, -1)
    cxo, cbo, cco = 0, D_INNER // N, D_INNER // N + G // P
    in_specs = [
        pl.BlockSpec((L, W), lambda g, c: (c, g)),
        pl.BlockSpec((L, W), lambda g, c: (c, xo + g)),
        pl.BlockSpec((HALO, W), halo(xo)),
        pl.BlockSpec((L, N), lambda g, c: (c, bo + g)),
        pl.BlockSpec((HALO, N), halo(bo)),
        pl.BlockSpec((L, N), lambda g, c: (c, co + g)),
        pl.BlockSpec((HALO, N), halo(co)),
        pl.BlockSpec((L, LANES), lambda g, c: (c, 0)),
        pl.BlockSpec((1, LANES), lambda g, c: (0, 0)),
        pl.BlockSpec((CONV_WIDTH, W), lambda g, c: (0, cxo + g)),
        pl.BlockSpec((1, W), lambda g, c: (0, cxo + g)),
        pl.BlockSpec((CONV_WIDTH, N), lambda g, c: (0, cbo + g)),
        pl.BlockSpec((1, N), lambda g, c: (0, cbo + g)),
        pl.BlockSpec((CONV_WIDTH, N), lambda g, c: (0, cco + g)),
        pl.BlockSpec((1, N), lambda g, c: (0, cco + g)),
        pl.BlockSpec((1, W), lambda g, c: (0, g)),
        pl.BlockSpec((1, W), lambda g, c: (0, g)),
    ]
    return pl.pallas_call(
        _ssd_kernel,
        grid=(G // P, nc),
        in_specs=in_specs,
        out_specs=pl.BlockSpec((L, W), lambda g, c: (c, g)),
        out_shape=jax.ShapeDtypeStruct((S, D_INNER), BF16),
        scratch_shapes=[pltpu.VMEM((P, SSM_STATE, GROUP_W), F32), pltpu.VMEM((LANES, L), F32),
                        pltpu.VMEM((CONV_WIDTH, L, L), BF16), pltpu.VMEM((P, LANES, LANES), BF16),
                        pltpu.VMEM((P, LANES, GROUP_W), BF16)],
        compiler_params=_cparams("parallel", "arbitrary"),
        name="ssd_chunk_scan",
    )(zx, zx, zx, zx, zx, zx, zx, dt, alog, cw, cb, cw, cb, cw, cb, dskip, norm_g)


def _attn_kernel(q_ref, k_ref, v_ref, lq1_ref, lk1_ref, lq2_ref, lk2_ref, sg_ref,
                 o_ref, m_ref, l_ref, acc_ref, p_ref, alpha_ref, *, lambda_init):
    qi = pl.program_id(1)
    T = ATT_T
    HD = DA_HEAD_DIM
    m_ref[...] = jnp.full_like(m_ref, -jnp.inf)
    l_ref[...] = jnp.zeros_like(l_ref)
    acc_ref[...] = jnp.zeros_like(acc_ref)

    def softmax_slab(kj, idx, masked):
        r0 = pl.multiple_of(kj * T, T)
        t = idx % 2
        q = q_ref[:, idx * HD:(idx + 1) * HD]
        kt = k_ref[pl.ds(r0, T), t * HD:(t + 1) * HD]
        s = lax.dot_general(q, kt, (((1,), (1,)), ((), ())), preferred_element_type=F32)
        if masked:
            row = lax.broadcasted_iota(jnp.int32, (T, T), 0)
            col = lax.broadcasted_iota(jnp.int32, (T, T), 1)
            s = jnp.where(col <= row, s, -jnp.inf)
        m_prev = m_ref[idx]
        m_new = jnp.maximum(m_prev, jnp.max(s, axis=-1, keepdims=True))
        alpha_ref[idx] = jnp.exp2(m_prev - m_new)
        p = jnp.exp2(s - _lane_tile(m_new, T // LANES))
        psum = p[:, 0:LANES]
        for b in range(1, T // LANES):
            psum = psum + p[:, b * LANES:(b + 1) * LANES]
        l_ref[idx] = alpha_ref[idx] * l_ref[idx] + psum
        p_ref[idx] = p.astype(BF16)
        m_ref[idx] = m_new

    def pv_slab(kj, idx):
        r0 = pl.multiple_of(kj * T, T)
        v = v_ref[pl.ds(r0, T), :]
        acc_ref[idx] = _lane_tile(alpha_ref[idx], DA_V_DIM // LANES) * acc_ref[idx] + _dot(p_ref[idx], v)

    n_slab = 2 * DA_GROUP
    for idx in range(n_slab):
        softmax_slab(qi, idx, True)

    def trip(prev, kj):
        for idx in range(n_slab):
            pv_slab(prev, idx)
            softmax_slab(kj, idx, False)

    rem = qi % ATT_UNROLL

    def single(kj, prev):
        trip(prev, kj)
        return kj

    def body(i, prev):
        k0 = rem + ATT_UNROLL * i
        trip(prev, k0)
        for u in range(1, ATT_UNROLL):
            trip(k0 + u - 1, k0 + u)
        return k0 + ATT_UNROLL - 1

    last = lax.fori_loop(0, qi // ATT_UNROLL, body, lax.fori_loop(0, rem, single, qi))
    for idx in range(n_slab):
        pv_slab(last, idx)

    lam = (jnp.exp(jnp.sum(lq1_ref[...] * lk1_ref[...], axis=-1, keepdims=True))
           - jnp.exp(jnp.sum(lq2_ref[...] * lk2_ref[...], axis=-1, keepdims=True)) + lambda_init)
    for gq in range(DA_GROUP):
        a0 = acc_ref[gq * 2] / jnp.sum(l_ref[gq * 2], axis=-1, keepdims=True)
        a1 = acc_ref[gq * 2 + 1] / jnp.sum(l_ref[gq * 2 + 1], axis=-1, keepdims=True)
        o = a0 - lam * a1
        ms = jnp.mean(o * o, axis=-1, keepdims=True)
        o = (o * lax.rsqrt(ms + EPS) * sg_ref[...]) * (1.0 - lambda_init)
        o_ref[:, gq * DA_V_DIM:(gq + 1) * DA_V_DIM] = o.astype(o_ref.dtype)


def _attn_call(q, kv, lq1, lk1, lq2, lk2, subln_g, lambda_init):
    S = q.shape[0]
    v_off = K_DIM // DA_V_DIM
    T = ATT_T
    qw = DA_GROUP * 2 * DA_HEAD_DIM
    kw = 2 * DA_HEAD_DIM
    vec = pl.BlockSpec((1, DA_HEAD_DIM), lambda h, i: (0, 0))
    return pl.pallas_call(
        functools.partial(_attn_kernel, lambda_init=lambda_init),
        grid=(DA_KV_HEADS, S // T),
        in_specs=[pl.BlockSpec((T, qw), lambda h, i: (i, h)),
                  pl.BlockSpec((S, kw), lambda h, i: (0, h)),
                  pl.BlockSpec((S, DA_V_DIM), lambda h, i: (0, v_off + h)),
                  vec, vec, vec, vec,
                  pl.BlockSpec((1, DA_V_DIM), lambda h, i: (0, 0))],
        out_specs=pl.BlockSpec((T, DA_GROUP * DA_V_DIM), lambda h, i: (i, h)),
        out_shape=jax.ShapeDtypeStruct((S, DA_HEADS * DA_V_DIM), BF16),
        scratch_shapes=[pltpu.VMEM((2 * DA_GROUP, T, LANES), F32),
                        pltpu.VMEM((2 * DA_GROUP, T, LANES), F32),
                        pltpu.VMEM((2 * DA_GROUP, T, DA_V_DIM), F32),
                        pltpu.VMEM((2 * DA_GROUP, T, T), BF16),
                        pltpu.VMEM((2 * DA_GROUP, T, LANES), F32)],
        compiler_params=_cparams("parallel", "parallel"),
        name="diff_attention",
    )(q, kv, kv, lq1, lk1, lq2, lk2, subln_g)


def _router_kernel(x_ref, g_ref, sh_ref, sc_ref, rw_ref, rb_ref, h_ref, idx_ref, wt_ref, cnt_ref, carry_ref,
                   *, n_real):
    i = pl.program_id(0)

    @pl.when(i == 0)
    def _():
        carry_ref[...] = jnp.zeros_like(carry_ref)

    @pl.when(i >= n_real)
    def _():
        h_ref[...] = jnp.zeros_like(h_ref)

    @pl.when(i < n_real)
    def _():
        _route(x_ref, g_ref, sh_ref, sc_ref, rw_ref, rb_ref, h_ref, idx_ref, wt_ref, cnt_ref, carry_ref)


def _route(x_ref, g_ref, sh_ref, sc_ref, rw_ref, rb_ref, h_ref, idx_ref, wt_ref, cnt_ref, carry_ref):
    h = _norm_mod(x_ref[...], g_ref[...], sh_ref[...], sc_ref[...])
    h_ref[...] = h.astype(BF16)
    hh = h.astype(BF16)
    hl = (h - hh.astype(F32)).astype(BF16)
    rw = rw_ref[...]
    wh = rw.astype(BF16)
    wl = (rw - wh.astype(F32)).astype(BF16)
    logits = _dot(hh, wh) + _dot(hl, wh) + _dot(hh, wl)
    lt = logits.T[0:N_EXPERTS, :]
    score = _sigmoid(lt)
    sel = score + rb_ref[...]
    srow = [score[e:e + 1, :] for e in range(N_EXPERTS)]
    row = [sel[e:e + 1, :] for e in range(N_EXPERTS)]

    gscore = []
    for gi in range(N_EXPERT_GROUPS):
        r = row[gi * EXPERTS_PER_GROUP:(gi + 1) * EXPERTS_PER_GROUP]
        best = None
        for a in range(EXPERTS_PER_GROUP):
            for b in range(a + 1, EXPERTS_PER_GROUP):
                s = r[a] + r[b]
                best = s if best is None else jnp.maximum(best, s)
        gscore.append(best)
    bestg = jnp.zeros_like(gscore[0], dtype=jnp.int32)
    bestv = gscore[0]
    for gi in range(1, N_EXPERT_GROUPS):
        upd = gscore[gi] > bestv
        bestg = jnp.where(upd, gi, bestg)
        bestv = jnp.where(upd, gscore[gi], bestv)

    cand = [jnp.where(bestg == e // EXPERTS_PER_GROUP, row[e], -jnp.inf) for e in range(N_EXPERTS)]

    def first_argmax(vals):
        bi = jnp.zeros_like(bestg)
        bv = vals[0]
        for e in range(1, N_EXPERTS):
            upd = vals[e] > bv
            bi = jnp.where(upd, e, bi)
            bv = jnp.where(upd, vals[e], bv)
        return bi

    i1 = first_argmax(cand)
    i2 = first_argmax([jnp.where(i1 == e, -jnp.inf, cand[e]) for e in range(N_EXPERTS)])
    s1 = sum(jnp.where(i1 == e, srow[e], 0.0) for e in range(N_EXPERTS))
    s2 = sum(jnp.where(i2 == e, srow[e], 0.0) for e in range(N_EXPERTS))
    den = s1 + s2
    tm = i1.shape[1]
    e_id = lax.broadcasted_iota(jnp.int32, (LANES, tm), 0)
    oh1 = jnp.where(e_id == i1, 1.0, 0.0)
    oh2 = jnp.where(e_id == i2, 1.0, 0.0)
    oh = (oh1 + oh2).T
    r_t = lax.broadcasted_iota(jnp.int32, (tm, tm), 0)
    c_t = lax.broadcasted_iota(jnp.int32, (tm, tm), 1)
    tril = jnp.where(r_t >= c_t, 1.0, 0.0).astype(BF16)
    incl = _dot(tril, oh.astype(BF16))
    carry = carry_ref[0:1, :]
    base = (incl - oh + carry).T
    rank1 = jnp.sum(oh1 * base, axis=0, keepdims=True).astype(jnp.int32)
    rank2 = jnp.sum(oh2 * base, axis=0, keepdims=True).astype(jnp.int32)
    total = carry + incl[tm - 1:tm, :]
    carry_ref[...] = jnp.broadcast_to(total, carry_ref.shape)
    cnt_ref[...] = jnp.broadcast_to(total, cnt_ref.shape).astype(jnp.int32)
    idx_ref[...] = jnp.concatenate([i1, i2, rank1, rank2, jnp.zeros((4, tm), jnp.int32)], axis=0)
    wt_ref[...] = jnp.concatenate([s1 / den, s2 / den, jnp.zeros((LANES - 2, tm), F32)], axis=0).T


def _router_call(x, g, sh, sc, rw_pad, rb_col, h_rows):
    S, D = x.shape
    tm = ROUTER_TM
    n_real = S // tm
    last = n_real - 1
    vec = pl.BlockSpec((1, D), lambda i: (0, 0))
    return pl.pallas_call(
        functools.partial(_router_kernel, n_real=n_real),
        grid=(h_rows // tm,),
        in_specs=[pl.BlockSpec((tm, D), lambda i: (jnp.minimum(i, last), 0)), vec, vec, vec,
                  pl.BlockSpec((D, LANES), lambda i: (0, 0)),
                  pl.BlockSpec((N_EXPERTS, 1), lambda i: (0, 0))],
        out_specs=[pl.BlockSpec((tm, D), lambda i: (i, 0)),
                   pl.BlockSpec((8, tm), lambda i: (0, jnp.minimum(i, last))),
                   pl.BlockSpec((tm, LANES), lambda i: (jnp.minimum(i, last), 0)),
                   pl.BlockSpec((8, LANES), lambda i: (0, 0))],
        out_shape=[jax.ShapeDtypeStruct((h_rows, D), BF16),
                   jax.ShapeDtypeStruct((8, S), jnp.int32),
                   jax.ShapeDtypeStruct((S, LANES), F32),
                   jax.ShapeDtypeStruct((8, LANES), jnp.int32)],
        scratch_shapes=[pltpu.VMEM((8, LANES), F32)],
        compiler_params=_cparams("arbitrary"),
        name="moe_norm_router",
    )(x, g, sh, sc, rw_pad, rb_col)


def _moe_kernel(ti_ref, te_ref, tf_ref, nv_ref, x_ref, wg_ref, wu_ref, wd_ref, o_ref,
                wgb_ref, wub_ref, wdb_ref):
    t = pl.program_id(0)

    @pl.when(t < nv_ref[0])
    def _():
        @pl.when(tf_ref[t] == 1)
        def _():
            wgb_ref[...] = wg_ref[...].astype(BF16)
            wub_ref[...] = wu_ref[...].astype(BF16)
            wdb_ref[...] = wd_ref[...].astype(BF16)

        x = x_ref[...]
        gt = _dot(x, wgb_ref[...])
        up = _dot(x, wub_ref[...])
        h = _silu(gt) * up
        o_ref[...] = _dot(h.astype(BF16), wdb_ref[...]).astype(o_ref.dtype)

    @pl.when(t >= nv_ref[0])
    def _():
        o_ref[...] = jnp.zeros_like(o_ref)


def _moe_call(xs, tile_idx, tile_expert, tile_first, n_valid, w_gate, w_up, w_down, layer):
    R, D = xs.shape
    tm = MOE_TM
    F = D_EXPERT
    grid_spec = pltpu.PrefetchScalarGridSpec(
        num_scalar_prefetch=4,
        grid=(R // tm,),
        in_specs=[pl.BlockSpec((tm, D), lambda t, ti, te, tf, nv: (ti[t], 0)),
                  pl.BlockSpec((None, None, D, F), lambda t, ti, te, tf, nv: (layer, te[t], 0, 0)),
                  pl.BlockSpec((None, None, D, F), lambda t, ti, te, tf, nv: (layer, te[t], 0, 0)),
                  pl.BlockSpec((None, None, F, D), lambda t, ti, te, tf, nv: (layer, te[t], 0, 0))],
        out_specs=pl.BlockSpec((tm, D), lambda t, ti, te, tf, nv: (t, 0)),
        scratch_shapes=[pltpu.VMEM((D, F), BF16), pltpu.VMEM((D, F), BF16), pltpu.VMEM((F, D), BF16)],
    )
    return pl.pallas_call(
        _moe_kernel,
        grid_spec=grid_spec,
        out_shape=jax.ShapeDtypeStruct((R, D), BF16),
        compiler_params=_cparams("arbitrary"),
        name="moe_experts",
    )(tile_idx, tile_expert, tile_first, n_valid, xs, w_gate, w_up, w_down)


def _combine_kernel(x_ref, y1_ref, y2_ref, wt_ref, g_ref, fg_ref, o_ref, *, final):
    wt = wt_ref[...]
    y = wt[:, 0:1] * y1_ref[...].astype(F32) + wt[:, 1:2] * y2_ref[...].astype(F32)
    x = x_ref[...] + g_ref[...] * y
    if final:
        ms = jnp.mean(x * x, axis=-1, keepdims=True)
        x = x * lax.rsqrt(ms + EPS) * fg_ref[...]
    o_ref[...] = x


def _combine_call(x, y1, y2, wts, gate, final_g, final, part):
    S, D = x.shape
    tm = COMBINE_TM
    off = part * (y1.shape[0] // tm)
    blk_x = pl.BlockSpec((tm, D), lambda i: (i + off, 0))
    blk_y = pl.BlockSpec((tm, D), lambda i: (i, 0))
    vec = pl.BlockSpec((1, D), lambda i: (0, 0))
    return pl.pallas_call(
        functools.partial(_combine_kernel, final=final),
        grid=(y1.shape[0] // tm,),
        in_specs=[blk_x, blk_y, blk_y, pl.BlockSpec((tm, LANES), lambda i: (i + off, 0)), vec, vec],
        out_specs=blk_x,
        out_shape=jax.ShapeDtypeStruct((S, D), F32),
        input_output_aliases={0: 0},
        compiler_params=_cparams("parallel"),
        name="moe_combine",
    )(x, y1, y2, wts, gate, final_g)


def _take_rows(a, rows):
    return a.at[rows].get(mode="promise_in_bounds")


def _moe_rows(S):
    return TOP_K * S + N_EXPERTS * MOE_TM


def _dispatch_plan(idx, counts, S):
    tm = MOE_TM
    R = _moe_rows(S)
    n_tiles = R // tm
    e_flat = idx[:TOP_K].reshape(-1)
    rank = idx[TOP_K:2 * TOP_K].reshape(-1)
    tok_flat = jnp.tile(jnp.arange(S, dtype=jnp.int32), TOP_K)
    onehot = (e_flat[:, None] == jnp.arange(N_EXPERTS, dtype=jnp.int32)[None, :]).astype(jnp.int32)
    tiles_e = (counts + tm - 1) // tm
    tile_end = jnp.cumsum(tiles_e)
    tile_start = tile_end - tiles_e
    dest = jnp.sum(onehot * (tile_start * tm)[None, :], axis=1) + rank
    src_tok = (jnp.arange(R, dtype=jnp.int32) % S).at[dest].set(
        tok_flat, mode="promise_in_bounds", unique_indices=True)
    n_valid = tile_end[-1]
    t_ids = jnp.minimum(jnp.arange(n_tiles, dtype=jnp.int32), n_valid - 1)
    tile_expert = jnp.sum((t_ids[:, None] >= tile_end[None, :]).astype(jnp.int32), axis=1)
    tile_first = jnp.sum(((t_ids[:, None] == tile_start[None, :]) & (tiles_e[None, :] > 0)).astype(jnp.int32),
                         axis=1)
    return (src_tok, t_ids, tile_expert, tile_first, n_valid.reshape(1).astype(jnp.int32), dest[:S], dest[S:])


def _moe_block(x, g, sh, sc, gate, rw_pad, rb_col, w_gate, w_up, w_down, layer, final_g, final):
    S = x.shape[0]
    h, idx, wts, cnt = _router_call(x, g, sh, sc, rw_pad, rb_col, _moe_rows(S) + ROUTER_TM)
    src_tok, t_ids, t_exp, t_first, n_valid, pos1, pos2 = _dispatch_plan(idx, cnt[0, :N_EXPERTS], S)
    xs = _take_rows(h, src_tok)
    ys = _moe_call(xs, t_ids, t_exp, t_first, n_valid, w_gate, w_up, w_down, layer)
    n = S // COMBINE_PARTS
    for part in range(COMBINE_PARTS):
        y1 = _take_rows(ys, pos1[part * n:(part + 1) * n])
        y2 = _take_rows(ys, pos2[part * n:(part + 1) * n])
        x = _combine_call(x, y1, y2, wts, gate, final_g, final, part)
    return x


def _rope_tables(S):
    inv = 1.0 / (ROPE_THETA ** (jnp.arange(0, ROPE_DIM, 2, dtype=F32) / ROPE_DIM))
    ang = jnp.arange(S, dtype=F32)[:, None] * inv[None, :]
    cos, sin = jnp.cos(ang), jnp.sin(ang)
    zeros = jnp.zeros((S, LANES - ROPE_DIM), F32)
    zh = jnp.zeros((S, ROPE_HALF), F32)
    cos_t = jnp.concatenate([cos, cos, jnp.ones((S, LANES - ROPE_DIM), F32)], axis=1)
    sin_a = jnp.concatenate([-sin, zh, zeros], axis=1)
    sin_b = jnp.concatenate([zh, sin, zeros], axis=1)
    return cos_t, sin_a, sin_b


def kernel(x, c, mod_w, mod_b, norm1_g, norm2_g, ssm_in_w, ssm_conv_w, ssm_conv_b, ssm_dt_bias, ssm_a_log, ssm_d, ssm_norm_g, ssm_out_w, kv_mod_w, kv_mod_b, kv_norm_g, w_kv, attn_q_w, lam_q1, lam_k1, lam_q2, lam_k2, subln_g, attn_o_w, moe_w_gate, moe_w_up, moe_w_down, router_w, router_b, final_g):
    _, S, D = x.shape
    xr = x[0]
    cb = jnp.broadcast_to(c[0][:, None], (D, LANES))
    mods = _mod_call(cb, mod_w, mod_b)
    kvm = _mod_call(cb, kv_mod_w[None], kv_mod_b[None])
    cos_t, sin_a, sin_b = _rope_tables(S)
    rw_pad = jnp.pad(router_w, ((0, 0), (0, LANES - N_EXPERTS)))
    rb_col = router_b.reshape(N_EXPERTS, 1)
    row = lambda v: v.reshape(1, -1)
    pad_l = lambda v: jnp.pad(v.reshape(1, -1), ((0, 0), (0, LANES - v.shape[-1])))

    kv = None
    for layer in range(DEPTH):
        m = mods[layer]
        sh1, sc1, g1, sh2, sc2, g2 = [m[:, i * D:(i + 1) * D] for i in range(N_MOD)]
        if layer == N_A_LAYERS:
            kv_sh, kv_sc = kvm[0][:, :D], kvm[0][:, D:]
            kv = _rope_proj_call(xr, row(kv_norm_g), kv_sh, kv_sc, w_kv.astype(BF16), cos_t, sin_a, sin_b,
                                 rope_cols=K_DIM, scale=1.0, name="kv_proj")
        if layer < N_A_LAYERS:
            i = layer
            w_in = ssm_in_w[i]
            zx, dt = _in_proj_call(xr, row(norm1_g[layer]), sh1, sc1,
                                   w_in.astype(BF16),
                                   jnp.pad(w_in[:, ZX_DIM:], ((0, 0), (0, LANES - SSM_HEADS))).astype(BF16),
                                   pad_l(ssm_dt_bias[i]))
            y = _ssd_call(zx, dt, pad_l(ssm_a_log[i]), ssm_conv_w[i], ssm_conv_b[i],
                          row(jnp.repeat(ssm_d[i], SSM_HEAD_DIM)), row(ssm_norm_g[i]))
            xr = _res_mm_call(y, ssm_out_w[i].astype(BF16), xr, g1, name="ssm_out_proj")
        else:
            j = layer - N_A_LAYERS
            lambda_init = 0.8 - 0.6 * float(np.exp(-0.3 * layer))
            q = _rope_proj_call(xr, row(norm1_g[layer]), sh1, sc1, attn_q_w[j].astype(BF16),
                                cos_t, sin_a, sin_b, rope_cols=D, scale=DA_HEAD_DIM ** -0.5 * LOG2E,
                                name="q_proj")
            o = _attn_call(q, kv, row(lam_q1[j]), row(lam_k1[j]), row(lam_q2[j]), row(lam_k2[j]),
                           row(subln_g[j]), lambda_init)
            xr = _res_mm_call(o, attn_o_w[j].astype(BF16), xr, g1, name="attn_o_proj")
        xr = _moe_block(xr, row(norm2_g[layer]), sh2, sc2, g2, rw_pad, rb_col,
                        moe_w_gate, moe_w_up, moe_w_down, layer, row(final_g), layer == DEPTH - 1)
    return xr[None]
```
